```python
import jax, jax.numpy as jnp
from jax import lax
import numpy as np

D_MODEL = 1024
BATCH = 16
SEQ = 2048
DEPTH = 4
DEC_BATCH = 32
DEC_SEQ = 32
PAST_LEN = 1024

CHUNK = 64
A_PAST_CHUNKS = 8
N_HEADS_A = 8
HEAD_DIM_A = 64
REL_MIN = -(CHUNK - 1)
REL_MAX = 2 * CHUNK
N_REL = REL_MAX - REL_MIN + 1
N_HEADS_B = 4
HEAD_K_B = 64
HEAD_V_B = 128
GATE_RANK = 16
GATE_NORM = 16.0
CONV_W = 31
D_FF = 4 * D_MODEL
N_AB_LAYERS = (DEPTH + 1) // 2
N_C_LAYERS = DEPTH // 2
EPS = 1e-6

WA = N_HEADS_A * HEAD_DIM_A
KB = N_HEADS_B * HEAD_K_B
VB = N_HEADS_B * HEAD_V_B
P_AB = 3 * WA + 2 * KB + 2 * VB + GATE_RANK

kernel_name = "chunk_band_gla_conformer_hybrid_step"


def rms_norm(x, g):
    xf = x.astype(jnp.float32)
    y = xf * lax.rsqrt(jnp.mean(xf * xf, axis=-1, keepdims=True) + EPS)
    return (y * g.astype(jnp.float32)).astype(x.dtype)


def layer_norm(x, g, b):
    xf = x.astype(jnp.float32)
    mu = jnp.mean(xf, axis=-1, keepdims=True)
    var = jnp.mean(jnp.square(xf - mu), axis=-1, keepdims=True)
    y = (xf - mu) * lax.rsqrt(var + EPS)
    return (y * g.astype(jnp.float32) + b.astype(jnp.float32)).astype(x.dtype)


def ab_project(xn, w_in, q_g, k_g, gate_w, gate_b):
    B, T, _ = xn.shape
    h = xn @ w_in
    o = np.cumsum([0, WA, WA, WA, KB, KB, VB, VB, GATE_RANK])
    qa, ka, va, qb, kb, vb, gb, lr = [h[..., o[n]:o[n + 1]] for n in range(8)]
    qa = rms_norm(qa.reshape(B, T, N_HEADS_A, HEAD_DIM_A), q_g)
    ka = rms_norm(ka.reshape(B, T, N_HEADS_A, HEAD_DIM_A), k_g)
    va = va.reshape(B, T, N_HEADS_A, HEAD_DIM_A)
    qb = qb.reshape(B, T, N_HEADS_B, HEAD_K_B) * (HEAD_K_B ** -0.5)
    kb = kb.reshape(B, T, N_HEADS_B, HEAD_K_B)
    vb = vb.reshape(B, T, N_HEADS_B, HEAD_V_B)
    gb = gb.reshape(B, T, N_HEADS_B, HEAD_V_B)
    logit = (lr @ gate_w + gate_b).astype(jnp.float32)
    la = (jax.nn.log_sigmoid(logit) / GATE_NORM).reshape(B, T, N_HEADS_B, HEAD_K_B)
    return qa, ka, va, qb, kb, vb, gb, la


def rel_bias(table, dist):
    idx = jnp.clip(dist, REL_MIN, REL_MAX) - REL_MIN
    return table[:, idx]


def attend(q, k, v, bias, valid=None):
    s = jnp.einsum('bqhd,bkhd->bhqk', q, k).astype(jnp.float32) * (HEAD_DIM_A ** -0.5)
    s = s + bias.astype(jnp.float32)[None]
    if valid is not None:
        s = jnp.where(valid, s, -1e30)
    p = jax.nn.softmax(s, axis=-1).astype(v.dtype)
    return jnp.einsum('bhqk,bkhd->bqhd', p, v)


def band_attn_prompt(q, k, v, table):
    B, S, H, dh = q.shape
    nc = S // CHUNK
    pad = A_PAST_CHUNKS * CHUNK
    band = pad + CHUNK
    kp = jnp.pad(k, ((0, 0), (pad, 0), (0, 0), (0, 0)))
    vp = jnp.pad(v, ((0, 0), (pad, 0), (0, 0), (0, 0)))
    qc = q.reshape(B, nc, CHUNK, H, dh).swapaxes(0, 1)
    r = jnp.arange(band)
    i = jnp.arange(CHUNK)
    bias = rel_bias(table, (pad + i)[:, None] - r[None, :])

    def one_chunk(args):
        c, qb = args
        kb = lax.dynamic_slice_in_dim(kp, c * CHUNK, band, axis=1)
        vb = lax.dynamic_slice_in_dim(vp, c * CHUNK, band, axis=1)
        valid = (c * CHUNK - pad + r) >= 0
        return attend(qb, kb, vb, bias, valid[None, None, None, :])

    out = lax.map(one_chunk, (jnp.arange(nc), qc))
    return out.swapaxes(0, 1).reshape(B, S, H, dh)


def band_attn_sample(q, k_new, v_new, ck, cv, table):
    T = q.shape[1]
    Lc = ck.shape[1]
    kk = jnp.concatenate([ck.astype(k_new.dtype), k_new], axis=1)
    vv = jnp.concatenate([cv.astype(v_new.dtype), v_new], axis=1)
    qpos = PAST_LEN + jnp.arange(T)
    kpos = jnp.concatenate([PAST_LEN - Lc + jnp.arange(Lc), PAST_LEN + jnp.arange(T)])
    bias = rel_bias(table, qpos[:, None] - kpos[None, :])
    return attend(q, kk, vv, bias)


def gla_chunk(S, inp):
    q, k, v, la = inp
    q = q.astype(jnp.float32)
    k = k.astype(jnp.float32)
    v = v.astype(jnp.float32)
    T = q.shape[1]
    b = jnp.cumsum(la, axis=1)
    causal = jnp.tril(jnp.ones((T, T), dtype=bool))
    diff = b[:, :, None] - b[:, None, :]
    decay = jnp.exp(jnp.where(causal[None, :, :, None, None], diff, -jnp.inf))
    attn = jnp.einsum('bihd,bjhd,bijhd->bhij', q, k, decay)
    o = jnp.einsum('bhij,bjhv->bihv', attn, v) + jnp.einsum('bihd,bhdv->bihv', q * jnp.exp(b), S)
    b_last = b[:, -1]
    S_new = jnp.exp(b_last)[..., None] * S + jnp.einsum(
        'bjhd,bjhv->bhdv', k * jnp.exp(b_last[:, None] - b), v)
    return S_new, o


def gla_prompt(q, k, v, la):
    B, S, H, dk = q.shape
    nc = S // CHUNK

    def to_chunks(a):
        return a.reshape(B, nc, CHUNK, H, a.shape[-1]).swapaxes(0, 1)

    S0 = jnp.zeros((B, H, dk, HEAD_V_B), jnp.float32)
    S_fin, o = lax.scan(gla_chunk, S0, (to_chunks(q), to_chunks(k), to_chunks(v), to_chunks(la)))
    return o.swapaxes(0, 1).reshape(B, S, H, HEAD_V_B), S_fin


def ab_merge(oa, ob, gb, onorm_g, w_out):
    B, T = oa.shape[:2]
    ob = rms_norm(ob.astype(gb.dtype), onorm_g) * jax.nn.silu(gb)
    cat = jnp.concatenate([oa.reshape(B, T, WA), ob.reshape(B, T, VB).astype(oa.dtype)], axis=-1)
    return cat @ w_out


def conv_module(xn, buf, w1, b1, w_dw, b_dw, ln_g, ln_b, w2, b2):
    u = xn @ w1 + b1
    u = u[..., :D_MODEL] * jax.nn.sigmoid(u[..., D_MODEL:])
    full = jnp.concatenate([buf.astype(u.dtype), u], axis=1)
    y = lax.conv_general_dilated(full, w_dw.astype(full.dtype)[:, None, :], window_strides=(1,),
                                 padding='VALID', dimension_numbers=('NWC', 'WIO', 'NWC'),
                                 feature_group_count=D_MODEL) + b_dw
    y = jax.nn.silu(layer_norm(y, ln_g, ln_b))
    return y @ w2 + b2, full[:, -(CONV_W - 1):]


def ffn(xn, w_up, w_down):
    return jnp.square(jax.nn.relu(xn @ w_up)) @ w_down


def setup_inputs(seed: int = 0) -> dict:
    key = jax.random.key(seed)
    ks = jax.random.split(key, 32)

    def nrm(k, shape, scale):
        return jax.random.normal(k, shape, jnp.float32) * scale

    la_len = min(A_PAST_CHUNKS * CHUNK, PAST_LEN)
    return {
        "x_prompt": nrm(ks[0], (BATCH, SEQ, D_MODEL), 1.0),
        "x_sample": nrm(ks[1], (DEC_BATCH, DEC_SEQ, D_MODEL), 1.0),
        "cache_a_k": nrm(ks[2], (N_AB_LAYERS, DEC_BATCH, la_len, N_HEADS_A, HEAD_DIM_A), 1.0),
        "cache_a_v": nrm(ks[3], (N_AB_LAYERS, DEC_BATCH, la_len, N_HEADS_A, HEAD_DIM_A), 1.0),
        "state_b": nrm(ks[4], (N_AB_LAYERS, DEC_BATCH, N_HEADS_B, HEAD_K_B, HEAD_V_B), 0.5),
        "cache_c": nrm(ks[5], (N_C_LAYERS, DEC_BATCH, CONV_W - 1, D_MODEL), 0.5),
        "norm_mix_g": 1.0 + nrm(ks[6], (DEPTH, D_MODEL), 0.02),
        "norm_ffn_g": 1.0 + nrm(ks[7], (DEPTH, D_MODEL), 0.02),
        "w_in_ab": nrm(ks[8], (N_AB_LAYERS, D_MODEL, P_AB), D_MODEL ** -0.5),
        "a_q_norm_g": 1.0 + nrm(ks[9], (N_AB_LAYERS, HEAD_DIM_A), 0.02),
        "a_k_norm_g": 1.0 + nrm(ks[10], (N_AB_LAYERS, HEAD_DIM_A), 0.02),
        "a_rel_bias": nrm(ks[11], (N_AB_LAYERS, N_HEADS_A, N_REL), 0.5),
        "b_gate_w_up": nrm(ks[12], (N_AB_LAYERS, GATE_RANK, KB), GATE_RANK ** -0.5),
        "b_gate_b": nrm(ks[13], (N_AB_LAYERS, KB), 0.1),
        "b_out_norm_g": 1.0 + nrm(ks[14], (N_AB_LAYERS, HEAD_V_B), 0.02),
        "w_out_ab": nrm(ks[15], (N_AB_LAYERS, D_MODEL, D_MODEL), D_MODEL ** -0.5),
        "c_w_pw1": nrm(ks[16], (N_C_LAYERS, D_MODEL, 2 * D_MODEL), D_MODEL ** -0.5),
        "c_b_pw1": nrm(ks[17], (N_C_LAYERS, 2 * D_MODEL), 0.02),
        "c_w_dw": nrm(ks[18], (N_C_LAYERS, CONV_W, D_MODEL), CONV_W ** -0.5),
        "c_b_dw": nrm(ks[19], (N_C_LAYERS, D_MODEL), 0.02),
        "c_ln_g": 1.0 + nrm(ks[20], (N_C_LAYERS, D_MODEL), 0.02),
        "c_ln_b": nrm(ks[21], (N_C_LAYERS, D_MODEL), 0.02),
        "c_w_pw2": nrm(ks[22], (N_C_LAYERS, D_MODEL, D_MODEL), D_MODEL ** -0.5),
        "c_b_pw2": nrm(ks[23], (N_C_LAYERS, D_MODEL), 0.02),
        "w_ffn_up": nrm(ks[24], (DEPTH, D_MODEL, D_FF), D_MODEL ** -0.5),
        "w_ffn_down": nrm(ks[25], (DEPTH, D_FF, D_MODEL), D_FF ** -0.5),
    }


def reference(x_prompt, x_sample, cache_a_k, cache_a_v, state_b, cache_c,
              norm_mix_g, norm_ffn_g, w_in_ab, a_q_norm_g, a_k_norm_g, a_rel_bias,
              b_gate_w_up, b_gate_b, b_out_norm_g, w_out_ab,
              c_w_pw1, c_b_pw1, c_w_dw, c_b_dw, c_ln_g, c_ln_b, c_w_pw2, c_b_pw2,
              w_ffn_up, w_ffn_down):
    xp, xs = x_prompt, x_sample
    akp, avp, aks, avs, sbp, sbs, ccp, ccs = [], [], [], [], [], [], [], []
    for l in range(DEPTH):
        hp = rms_norm(xp, norm_mix_g[l])
        hs = rms_norm(xs, norm_mix_g[l])
        if l % 2 == 0:
            i = l // 2
            proj = (w_in_ab[i], a_q_norm_g[i], a_k_norm_g[i], b_gate_w_up[i], b_gate_b[i])
            qa, ka, va, qb, kb, vb, gb, la = ab_project(hp, *proj)
            oa = band_attn_prompt(qa, ka, va, a_rel_bias[i])
            ob, sb = gla_prompt(qb, kb, vb, la)
            mp = ab_merge(oa, ob, gb, b_out_norm_g[i], w_out_ab[i])
            lp = min(A_PAST_CHUNKS * CHUNK, xp.shape[1])
            akp.append(ka[:, -lp:])
            avp.append(va[:, -lp:])
            sbp.append(sb)
            qa, ka, va, qb, kb, vb, gb, la = ab_project(hs, *proj)
            oa = band_attn_sample(qa, ka, va, cache_a_k[i], cache_a_v[i], a_rel_bias[i])
            sb, ob = gla_chunk(state_b[i].astype(jnp.float32), (qb, kb, vb, la))
            ms = ab_merge(oa, ob, gb, b_out_norm_g[i], w_out_ab[i])
            aks.append(ka)
            avs.append(va)
            sbs.append(sb)
        else:
            j = l // 2
            cp = (c_w_pw1[j], c_b_pw1[j], c_w_dw[j], c_b_dw[j], c_ln_g[j], c_ln_b[j],
                  c_w_pw2[j], c_b_pw2[j])
            zero_buf = jnp.zeros((xp.shape[0], CONV_W - 1, D_MODEL), hp.dtype)
            mp, bp = conv_module(hp, zero_buf, *cp)
            ms, bs = conv_module(hs, cache_c[j], *cp)
            ccp.append(bp)
            ccs.append(bs)
        xp = xp + mp.astype(xp.dtype)
        xs = xs + ms.astype(xs.dtype)
        xp = xp + ffn(rms_norm(xp, norm_ffn_g[l]), w_ffn_up[l], w_ffn_down[l]).astype(xp.dtype)
        xs = xs + ffn(rms_norm(xs, norm_ffn_g[l]), w_ffn_up[l], w_ffn_down[l]).astype(xs.dtype)
    return (xp, xs,
            jnp.stack(akp), jnp.stack(avp), jnp.stack(aks), jnp.stack(avs),
            jnp.stack(sbp), jnp.stack(sbs), jnp.stack(ccp), jnp.stack(ccs))
```

```python
import functools

import numpy as np
import jax
import jax.numpy as jnp
from jax import lax
from jax.experimental import pallas as pl
from jax.experimental.pallas import tpu as pltpu

F32 = jnp.float32
BF16 = jnp.bfloat16

EPS = 1e-6
CHUNK = 64
PAST_CHUNKS = 8
PAD = PAST_CHUNKS * CHUNK
BAND = PAD + CHUNK
N_HEADS_A = 8
HEAD_DIM_A = 64
REL_MIN = -(CHUNK - 1)
REL_MAX = 2 * CHUNK
N_HEADS_B = 4
HEAD_K_B = 64
HEAD_V_B = 128
GATE_RANK = 16
GATE_NORM = 16.0
CONV_W = 31
CONV_HALO = 32
WA = N_HEADS_A * HEAD_DIM_A
KB = N_HEADS_B * HEAD_K_B
VB = N_HEADS_B * HEAD_V_B
LANES = 128
VMEM_LIMIT = 56 * 1024 * 1024


def _mm(a, b):
    return jnp.dot(a, b, preferred_element_type=F32)


def _mm_nt(a, b):
    return lax.dot_general(a, b, (((1,), (1,)), ((), ())), preferred_element_type=F32)


def _mm_tn(a, b):
    return lax.dot_general(a, b, (((0,), (0,)), ((), ())), preferred_element_type=F32)


def _rms(x, g):
    ms = jnp.mean(x * x, axis=-1, keepdims=True)
    return x * lax.rsqrt(ms + EPS) * g


def _sigmoid(x):
    return 1.0 / (1.0 + jnp.exp(-x))


def _params(*sem):
    return pltpu.CompilerParams(dimension_semantics=sem, vmem_limit_bytes=VMEM_LIMIT)


def _row_tile(rows, want):
    tm = min(rows, want)
    assert rows % tm == 0, (rows, tm)
    return tm


def _ffn_kernel(x_ref, g_ref, wu_ref, wd_ref, o_ref, xn_scr, acc_scr):
    j = pl.program_id(1)

    @pl.when(j == 0)
    def _():
        xn_scr[...] = _rms(x_ref[...], g_ref[...]).astype(BF16)
        acc_scr[...] = jnp.zeros_like(acc_scr)

    h = jnp.maximum(_mm(xn_scr[...], wu_ref[...]), 0.0)
    acc_scr[...] += _mm((h * h).astype(BF16), wd_ref[...])

    @pl.when(j == pl.num_programs(1) - 1)
    def _():
        o_ref[...] = x_ref[...] + acc_scr[...]


def _ffn(x, g, w_up, w_down, *, tm=1024, tf=1024):
    rows, d = x.shape
    dff = w_up.shape[1]
    tm = _row_tile(rows, tm)
    return pl.pallas_call(
        _ffn_kernel,
        grid=(rows // tm, dff // tf),
        in_specs=[
            pl.BlockSpec((tm, d), lambda i, j: (i, 0)),
            pl.BlockSpec((1, d), lambda i, j: (0, 0)),
            pl.BlockSpec((d, tf), lambda i, j: (0, j)),
            pl.BlockSpec((tf, d), lambda i, j: (j, 0)),
        ],
        out_specs=pl.BlockSpec((tm, d), lambda i, j: (i, 0)),
        out_shape=jax.ShapeDtypeStruct((rows, d), F32),
        scratch_shapes=[pltpu.VMEM((tm, d), BF16), pltpu.VMEM((tm, d), F32)],
        compiler_params=_params("parallel", "arbitrary"),
        name="ffn",
    )(x, g, w_up, w_down)


def _inproj_kernel(x_ref, g_ref, w_ref, wlr_ref, gw_ref, gbias_ref, qg_ref, kg_ref, bd_ref,
                   q_o, k_o, v_o, qb_o, kb_o, la_o, vb_o, gb_o):
    xn = _rms(x_ref[...], g_ref[...]).astype(BF16)

    def seg(a, b):
        return _mm(xn, w_ref[:, a:b])

    def head_norm(h, gain):
        ss = _mm((h * h).astype(BF16), bd_ref[...]) * (1.0 / HEAD_DIM_A)
        return h * lax.rsqrt(ss + EPS) * gain

    q_o[...] = (head_norm(seg(0, WA), qg_ref[...]) * (HEAD_DIM_A ** -0.5)).astype(BF16)
    k_o[...] = head_norm(seg(WA, 2 * WA), kg_ref[...])
    v_o[...] = seg(2 * WA, 3 * WA)
    o = 3 * WA
    hb = seg(o, o + 2 * KB)
    qb_o[...] = hb[:, :KB] * (HEAD_K_B ** -0.5)
    kb_o[...] = hb[:, KB:]
    vb_o[...] = seg(o + 2 * KB, o + 2 * KB + VB).astype(BF16)
    gb_o[...] = seg(o + 2 * KB + VB, o + 2 * KB + 2 * VB)
    lr = _mm(xn, wlr_ref[...])
    logit = _mm(lr.astype(BF16), gw_ref[...]) + gbias_ref[...]
    log_sig = jnp.minimum(logit, 0.0) - jnp.log1p(jnp.exp(-jnp.abs(logit)))
    la_o[...] = log_sig * (1.0 / GATE_NORM)


def _inproj(x, g, w_main, w_lr, gate_w, gate_b, qg, kg, bd, *, tm=512):
    rows, d = x.shape
    tm = _row_tile(rows, tm)
    row = lambda w: pl.BlockSpec((tm, w), lambda i: (i, 0))
    full = lambda a: pl.BlockSpec(a.shape, lambda i: (0,) * a.ndim)
    outs = [(WA, BF16), (WA, F32), (WA, F32), (KB, F32), (KB, F32), (KB, F32), (VB, BF16), (VB, F32)]
    return pl.pallas_call(
        _inproj_kernel,
        grid=(rows // tm,),
        in_specs=[row(d), full(g), full(w_main), full(w_lr), full(gate_w), full(gate_b),
                  full(qg), full(kg), full(bd)],
        out_specs=[row(w) for w, _ in outs],
        out_shape=[jax.ShapeDtypeStruct((rows, w), dt) for w, dt in outs],
        compiler_params=_params("parallel"),
        name="inproj",
    )(x, g, w_main, w_lr, gate_w, gate_b, qg, kg, bd)


def _softmax_pv(scores, values):
    m = scores[0].max(axis=-1, keepdims=True)
    for s in scores[1:]:
        m = jnp.maximum(m, s.max(axis=-1, keepdims=True))
    l = 0.0
    pv = 0.0
    for s, v in zip(scores, values):
        p = jnp.exp(s - m)
        l = l + p.sum(axis=-1, keepdims=True)
        pv = pv + _mm(p.astype(BF16), v)
    return pv / l


def _attn_prompt_kernel(q_ref, k_ref, v_ref, bias_ref, o_ref, kp_scr, vp_scr, *, nc):
    t = nc * CHUNK
    zeros = jnp.zeros((PAD, WA), BF16)
    kp_scr[0:PAD, :] = zeros
    vp_scr[0:PAD, :] = zeros
    kp_scr[PAD:PAD + t, :] = k_ref[...].astype(BF16)
    vp_scr[PAD:PAD + t, :] = v_ref[...].astype(BF16)
    first_half = lax.broadcasted_iota(jnp.int32, (CHUNK, LANES), 1) < HEAD_DIM_A
    col = lax.broadcasted_iota(jnp.int32, (CHUNK, BAND), 1)

    def chunk(c, carry):
        r0 = pl.multiple_of(c * CHUNK, CHUNK)
        valid = col >= PAD - c * CHUNK
        for p in range(N_HEADS_A // 2):
            lanes = slice(p * LANES, (p + 1) * LANES)
            q2 = q_ref[pl.ds(r0, CHUNK), lanes]
            kw = kp_scr[pl.ds(r0, BAND), lanes]
            vw = vp_scr[pl.ds(r0, BAND), lanes]
            outs = []
            for e in range(2):
                qm = jnp.where(first_half if e == 0 else ~first_half, q2, jnp.zeros_like(q2))
                s = _mm_nt(qm, kw) + bias_ref[2 * p + e]
                s = jnp.where(valid, s, -1e30)
                outs.append(_softmax_pv([s], [vw]))
            o_ref[pl.ds(r0, CHUNK), lanes] = jnp.where(first_half, outs[0], outs[1]).astype(BF16)
        return carry

    lax.fori_loop(0, nc, chunk, 0)


def _attn_prompt(q, k, v, bias, nb, t):
    nc = t // CHUNK
    blk = pl.BlockSpec((t, WA), lambda b: (b, 0))
    return pl.pallas_call(
        functools.partial(_attn_prompt_kernel, nc=nc),
        grid=(nb,),
        in_specs=[blk, blk, blk, pl.BlockSpec(bias.shape, lambda b: (0, 0, 0))],
        out_specs=blk,
        out_shape=jax.ShapeDtypeStruct((nb * t, WA), BF16),
        scratch_shapes=[pltpu.VMEM((PAD + t, WA), BF16), pltpu.VMEM((PAD + t, WA), BF16)],
        compiler_params=_params("parallel"),
        name="attn_prompt",
    )(q, k, v, bias)


def _attn_sample_kernel(q_ref, kn_ref, vn_ref, ck_ref, cv_ref, bias_c_ref, bias_n_ref, o_ref):
    t = q_ref.shape[0]
    first_half = lax.broadcasted_iota(jnp.int32, (t, LANES), 1) < HEAD_DIM_A
    for p in range(N_HEADS_A // 2):
        lanes = slice(p * LANES, (p + 1) * LANES)
        q2 = q_ref[:, lanes]
        kc = ck_ref[0, :, lanes].astype(BF16)
        vc = cv_ref[0, :, lanes].astype(BF16)
        kn = kn_ref[:, lanes].astype(BF16)
        vn = vn_ref[:, lanes].astype(BF16)
        outs = []
        for e in range(2):
            qm = jnp.where(first_half if e == 0 else ~first_half, q2, jnp.zeros_like(q2))
            s_c = _mm_nt(qm, kc) + bias_c_ref[2 * p + e]
            s_n = _mm_nt(qm, kn) + bias_n_ref[2 * p + e]
            outs.append(_softmax_pv([s_c, s_n], [vc, vn]))
        o_ref[:, lanes] = jnp.where(first_half, outs[0], outs[1]).astype(BF16)


def _attn_sample(q, k_new, v_new, cache_k, cache_v, bias_c, bias_n, nb, t):
    lc = cache_k.shape[1]
    blk = pl.BlockSpec((t, WA), lambda b: (b, 0))
    cblk = pl.BlockSpec((1, lc, WA), lambda b: (b, 0, 0))
    full = lambda a: pl.BlockSpec(a.shape, lambda b: (0,) * a.ndim)
    return pl.pallas_call(
        _attn_sample_kernel,
        grid=(nb,),
        in_specs=[blk, blk, blk, cblk, cblk, full(bias_c), full(bias_n)],
        out_specs=blk,
        out_shape=jax.ShapeDtypeStruct((nb * t, WA), BF16),
        compiler_params=_params("parallel"),
        name="attn_sample",
    )(q, k_new, v_new, cache_k, cache_v, bias_c, bias_n)


def _gla_kernel(q_ref, k_ref, v_ref, la_ref, s0_ref, o_ref, sout_ref, st_scr, *, nc, c_len):
    n_pairs = N_HEADS_B // 2
    for p in range(n_pairs):
        st_scr[p] = s0_ref[0, p * LANES:(p + 1) * LANES, :].T
    ri = lax.broadcasted_iota(jnp.int32, (c_len, c_len), 0)
    ci = lax.broadcasted_iota(jnp.int32, (c_len, c_len), 1)
    causal = ri >= ci
    tril = causal.astype(BF16)
    first_half = lax.broadcasted_iota(jnp.int32, (c_len, LANES), 1) < HEAD_K_B
    first_half_sq = lax.broadcasted_iota(jnp.int32, (LANES, LANES), 1) < HEAD_K_B

    def chunk(c, carry):
        r0 = pl.multiple_of(c * c_len, c_len)
        la = la_ref[pl.ds(r0, c_len), :]
        la_hi = la.astype(BF16)
        la_lo = (la - la_hi.astype(F32)).astype(BF16)
        b = _mm(tril, la_hi) + _mm(tril, la_lo)
        b_last = b[c_len - 1:c_len, :]
        ref = 0.5 * b_last
        q = q_ref[pl.ds(r0, c_len), :]
        k = k_ref[pl.ds(r0, c_len), :]
        qt = (q * jnp.exp(b - ref)).astype(BF16)
        kt = (k * jnp.exp(ref - b)).astype(BF16)
        qd = (q * jnp.exp(b)).astype(BF16)
        kd = (k * jnp.exp(b_last - b)).astype(BF16)
        dl = jnp.exp(b_last)
        for p in range(n_pairs):
            lanes = slice(p * LANES, (p + 1) * LANES)
            st = st_scr[p]
            st_b = st.astype(BF16)
            kt2 = kt[:, lanes]
            kd2 = kd[:, lanes]
            upd = []
            for e in range(2):
                h = 2 * p + e
                hm = first_half if e == 0 else ~first_half
                qt_m = jnp.where(hm, qt[:, lanes], jnp.zeros_like(kt2))
                qd_m = jnp.where(hm, qd[:, lanes], jnp.zeros_like(kt2))
                a = jnp.where(causal, _mm_nt(qt_m, kt2), 0.0).astype(BF16)
                vh = v_ref[pl.ds(r0, c_len), h * HEAD_V_B:(h + 1) * HEAD_V_B]
                o_ref[pl.ds(r0, c_len), h * HEAD_V_B:(h + 1) * HEAD_V_B] = _mm(a, vh) + _mm_nt(qd_m, st_b)
                upd.append(_mm_tn(vh, kd2))
            st_scr[p] = st * dl[:, lanes] + jnp.where(first_half_sq, upd[0], upd[1])
        return carry

    lax.fori_loop(0, nc, chunk, 0)
    for p in range(n_pairs):
        sout_ref[0, p * LANES:(p + 1) * LANES, :] = st_scr[p].T


def _gla(q, k, v, la, s0, nb, t, c_len):
    nc = t // c_len
    kblk = pl.BlockSpec((t, KB), lambda b: (b, 0))
    vblk = pl.BlockSpec((t, VB), lambda b: (b, 0))
    sblk = pl.BlockSpec((1, KB, HEAD_V_B), lambda b: (b, 0, 0))
    return pl.pallas_call(
        functools.partial(_gla_kernel, nc=nc, c_len=c_len),
        grid=(nb,),
        in_specs=[kblk, kblk, vblk, kblk, sblk],
        out_specs=[vblk, sblk],
        out_shape=[jax.ShapeDtypeStruct((nb * t, VB), F32),
                   jax.ShapeDtypeStruct((nb, KB, HEAD_V_B), F32)],
        scratch_shapes=[pltpu.VMEM((N_HEADS_B // 2, LANES, LANES), F32)],
        compiler_params=_params("parallel"),
        name="gla",
    )(q, k, v, la, s0)


def _merge_kernel(x_ref, oa_ref, ob_ref, gb_ref, og_ref, wa_ref, wb_ref, o_ref):
    parts = []
    for h in range(N_HEADS_B):
        lanes = slice(h * HEAD_V_B, (h + 1) * HEAD_V_B)
        gb = gb_ref[:, lanes]
        parts.append((_rms(ob_ref[:, lanes], og_ref[...]) * (gb * _sigmoid(gb))).astype(BF16))
    obn = jnp.concatenate(parts, axis=1)
    o_ref[...] = x_ref[...] + _mm(oa_ref[...], wa_ref[...]) + _mm(obn, wb_ref[...])


def _merge(x, oa, ob, gb, og, w_a, w_b, *, tm=512):
    rows, d = x.shape
    tm = _row_tile(rows, tm)
    row = lambda w: pl.BlockSpec((tm, w), lambda i: (i, 0))
    full = lambda a: pl.BlockSpec(a.shape, lambda i: (0,) * a.ndim)
    return pl.pallas_call(
        _merge_kernel,
        grid=(rows // tm,),
        in_specs=[row(d), row(WA), row(VB), row(VB), full(og), full(w_a), full(w_b)],
        out_specs=row(d),
        out_shape=jax.ShapeDtypeStruct((rows, d), F32),
        compiler_params=_params("parallel"),
        name="merge",
    )(x, oa, ob, gb, og, w_a, w_b)


def _conv_kernel(x_ref, init_ref, g_ref, w1_ref, b1_ref, wdw_ref, bdw_ref, lng_ref, lnb_ref,
                 w2_ref, b2_ref, o_ref, tail_ref, buf_scr, *, tt):
    d = x_ref.shape[1]
    lo = CONV_HALO - (CONV_W - 1)

    @pl.when(pl.program_id(1) == 0)
    def _():
        buf_scr[lo:CONV_HALO, :] = init_ref[0]

    x = x_ref[...]
    u = _mm(_rms(x, g_ref[...]).astype(BF16), w1_ref[...]) + b1_ref[...]
    buf_scr[CONV_HALO:CONV_HALO + tt, :] = u[:, :d] * _sigmoid(u[:, d:])
    y = bdw_ref[...]
    for kk in range(CONV_W):
        y = y + wdw_ref[kk:kk + 1, :] * buf_scr[lo + kk:lo + kk + tt, :]
    mu = jnp.mean(y, axis=-1, keepdims=True)
    yc = y - mu
    var = jnp.mean(yc * yc, axis=-1, keepdims=True)
    yn = yc * lax.rsqrt(var + EPS) * lng_ref[...] + lnb_ref[...]
    z = (yn * _sigmoid(yn)).astype(BF16)
    o_ref[...] = x + _mm(z, w2_ref[...]) + b2_ref[...]
    tail = buf_scr[tt:tt + CONV_HALO, :]
    tail_ref[0] = tail
    buf_scr[0:CONV_HALO, :] = tail


def _conv(x, init, g, w1, b1, wdw, bdw, lng, lnb, w2, b2, nb, t, *, tt=256):
    rows, d = x.shape
    tt = _row_tile(t, tt)
    nt = t // tt
    full = lambda a: pl.BlockSpec(a.shape, lambda b, i: (0,) * a.ndim)
    xblk = pl.BlockSpec((tt, d), lambda b, i: (b * nt + i, 0))
    return pl.pallas_call(
        functools.partial(_conv_kernel, tt=tt),
        grid=(nb, nt),
        in_specs=[xblk, pl.BlockSpec((1, CONV_W - 1, d), lambda b, i: (b, 0, 0)),
                  full(g), full(w1), full(b1), full(wdw), full(bdw), full(lng), full(lnb),
                  full(w2), full(b2)],
        out_specs=[xblk, pl.BlockSpec((1, CONV_HALO, d), lambda b, i: (b, 0, 0))],
        out_shape=[jax.ShapeDtypeStruct((rows, d), F32),
                   jax.ShapeDtypeStruct((nb, CONV_HALO, d), F32)],
        scratch_shapes=[pltpu.VMEM((CONV_HALO + tt, d), F32)],
        compiler_params=_params("parallel", "arbitrary"),
        name="conv",
    )(x, init, g, w1, b1, wdw, bdw, lng, lnb, w2, b2)


def _band_bias(table):
    i = np.arange(CHUNK)[:, None]
    r = np.arange(BAND)[None, :]
    idx = np.clip(PAD + i - r, REL_MIN, REL_MAX) - REL_MIN
    return table[:, idx]


def kernel(x_prompt, x_sample, cache_a_k, cache_a_v, state_b, cache_c, norm_mix_g, norm_ffn_g, w_in_ab, a_q_norm_g, a_k_norm_g, a_rel_bias, b_gate_w_up, b_gate_b, b_out_norm_g, w_out_ab, c_w_pw1, c_b_pw1, c_w_dw, c_b_dw, c_ln_g, c_ln_b, c_w_pw2, c_b_pw2, w_ffn_up, w_ffn_down):
    nbp, tp, d = x_prompt.shape
    nbs, ts, _ = x_sample.shape
    depth = norm_mix_g.shape[0]
    lp = min(PAD, tp)
    lc = cache_a_k.shape[2]
    assert lc == PAD and ts <= CHUNK and tp % CHUNK == 0
    xp = x_prompt.reshape(nbp * tp, d)
    xs = x_sample.reshape(nbs * ts, d)
    row = lambda a: a.reshape(1, -1)

    head_id = np.arange(WA) // HEAD_DIM_A
    bd = jnp.asarray(head_id[:, None] == head_id[None, :], BF16)
    p_main = 3 * WA + 2 * KB + 2 * VB

    akp, avp, aks, avs, sbp, sbs, ccp, ccs = [], [], [], [], [], [], [], []
    for l in range(depth):
        g_mix = row(norm_mix_g[l])
        if l % 2 == 0:
            i = l // 2
            w_in = w_in_ab[i]
            w_main = w_in[:, :p_main].astype(BF16)
            w_lr = jnp.pad(w_in[:, p_main:], ((0, 0), (0, LANES - GATE_RANK))).astype(BF16)
            gate_w = jnp.pad(b_gate_w_up[i], ((0, LANES - GATE_RANK), (0, 0))).astype(BF16)
            proj = (g_mix, w_main, w_lr, gate_w, row(b_gate_b[i]),
                    row(jnp.tile(a_q_norm_g[i], N_HEADS_A)), row(jnp.tile(a_k_norm_g[i], N_HEADS_A)), bd)
            bias = _band_bias(a_rel_bias[i])
            og = row(b_out_norm_g[i])
            w_a = w_out_ab[i][:WA].astype(BF16)
            w_b = w_out_ab[i][WA:].astype(BF16)

            q, k, v, qb, kb, la, vb, gb = _inproj(xp, *proj)
            oa = _attn_prompt(q, k, v, bias, nbp, tp)
            ob, sb = _gla(qb, kb, vb, la, jnp.zeros((nbp, KB, HEAD_V_B), F32), nbp, tp, CHUNK)
            akp.append(k.reshape(nbp, tp, N_HEADS_A, HEAD_DIM_A)[:, tp - lp:])
            avp.append(v.reshape(nbp, tp, N_HEADS_A, HEAD_DIM_A)[:, tp - lp:])
            sbp.append(sb.reshape(nbp, N_HEADS_B, HEAD_K_B, HEAD_V_B))
            xp = _merge(xp, oa, ob, gb, og, w_a, w_b)

            q, k, v, qb, kb, la, vb, gb = _inproj(xs, *proj)
            oa = _attn_sample(q, k, v, cache_a_k[i].reshape(nbs, lc, WA), cache_a_v[i].reshape(nbs, lc, WA),
                              bias[:, :ts, :lc], bias[:, :ts, lc:lc + ts], nbs, ts)
            ob, sb = _gla(qb, kb, vb, la, state_b[i].reshape(nbs, KB, HEAD_V_B), nbs, ts, ts)
            aks.append(k.reshape(nbs, ts, N_HEADS_A, HEAD_DIM_A))
            avs.append(v.reshape(nbs, ts, N_HEADS_A, HEAD_DIM_A))
            sbs.append(sb.reshape(nbs, N_HEADS_B, HEAD_K_B, HEAD_V_B))
            xs = _merge(xs, oa, ob, gb, og, w_a, w_b)
        else:
            j = l // 2
            cw = (g_mix, c_w_pw1[j].astype(BF16), row(c_b_pw1[j]), c_w_dw[j], row(c_b_dw[j]),
                  row(c_ln_g[j]), row(c_ln_b[j]), c_w_pw2[j].astype(BF16), row(c_b_pw2[j]))
            xp, tail = _conv(xp, jnp.zeros((nbp, CONV_W - 1, d), F32), *cw, nbp, tp)
            ccp.append(tail[:, CONV_HALO - (CONV_W - 1):])
            xs, tail = _conv(xs, cache_c[j], *cw, nbs, ts)
            ccs.append(tail[:, CONV_HALO - (CONV_W - 1):])
        w_up = w_ffn_up[l].astype(BF16)
        w_down = w_ffn_down[l].astype(BF16)
        g_ffn = row(norm_ffn_g[l])
        xp = _ffn(xp, g_ffn, w_up, w_down)
        xs = _ffn(xs, g_ffn, w_up, w_down)
    return (xp.reshape(nbp, tp, d), xs.reshape(nbs, ts, d),
            jnp.stack(akp), jnp.stack(avp), jnp.stack(aks), jnp.stack(avs),
            jnp.stack(sbp), jnp.stack(sbs), jnp.stack(ccp), jnp.stack(ccs))
```

```python
import functools

import numpy as np
import jax
import jax.numpy as jnp
from jax import lax
from jax.experimental import pallas as pl
from jax.experimental.pallas import tpu as pltpu

F32 = jnp.float32
BF16 = jnp.bfloat16

EPS = 1e-6
CHUNK = 64
PAST_CHUNKS = 8
PAD = PAST_CHUNKS * CHUNK
BAND = PAD + CHUNK
N_HEADS_A = 8
HEAD_DIM_A = 64
REL_MIN = -(CHUNK - 1)
REL_MAX = 2 * CHUNK
N_HEADS_B = 4
HEAD_K_B = 64
HEAD_V_B = 128
GATE_RANK = 16
GATE_NORM = 16.0
CONV_W = 31
CONV_HALO = 32
WA = N_HEADS_A * HEAD_DIM_A
KB = N_HEADS_B * HEAD_K_B
VB = N_HEADS_B * HEAD_V_B
LANES = 128
VMEM_LIMIT = 56 * 1024 * 1024


def _mm(a, b):
    return jnp.dot(a, b, preferred_element_type=F32)


def _mm_nt(a, b):
    return lax.dot_general(a, b, (((1,), (1,)), ((), ())), preferred_element_type=F32)


def _mm_tn(a, b):
    return lax.dot_general(a, b, (((0,), (0,)), ((), ())), preferred_element_type=F32)


def _rms(x, g):
    ms = jnp.mean(x * x, axis=-1, keepdims=True)
    return x * lax.rsqrt(ms + EPS) * g


def _sigmoid(x):
    return 1.0 / (1.0 + jnp.exp(-x))


def _params(*sem):
    return pltpu.CompilerParams(dimension_semantics=sem, vmem_limit_bytes=VMEM_LIMIT)


def _row_tile(rows, want):
    tm = min(rows, want)
    assert rows % tm == 0, (rows, tm)
    return tm


def _ffn_kernel(x_ref, g_ref, wu_ref, wd_ref, o_ref, xn_scr, acc_scr):
    j = pl.program_id(1)

    @pl.when(j == 0)
    def _():
        xn_scr[...] = _rms(x_ref[...], g_ref[...]).astype(BF16)
        acc_scr[...] = jnp.zeros_like(acc_scr)

    h = jnp.maximum(_mm(xn_scr[...], wu_ref[...]), 0.0)
    acc_scr[...] += _mm((h * h).astype(BF16), wd_ref[...])

    @pl.when(j == pl.num_programs(1) - 1)
    def _():
        o_ref[...] = x_ref[...] + acc_scr[...]


def _ffn(x, g, w_up, w_down, *, tm=1024, tf=1024):
    rows, d = x.shape
    dff = w_up.shape[1]
    tm = _row_tile(rows, tm)
    return pl.pallas_call(
        _ffn_kernel,
        grid=(rows // tm, dff // tf),
        in_specs=[
            pl.BlockSpec((tm, d), lambda i, j: (i, 0)),
            pl.BlockSpec((1, d), lambda i, j: (0, 0)),
            pl.BlockSpec((d, tf), lambda i, j: (0, j)),
            pl.BlockSpec((tf, d), lambda i, j: (j, 0)),
        ],
        out_specs=pl.BlockSpec((tm, d), lambda i, j: (i, 0)),
        out_shape=jax.ShapeDtypeStruct((rows, d), F32),
        scratch_shapes=[pltpu.VMEM((tm, d), BF16), pltpu.VMEM((tm, d), F32)],
        compiler_params=_params("parallel", "arbitrary"),
        name="ffn",
    )(x, g, w_up, w_down)


def _inproj_kernel(x_ref, g_ref, w_ref, wlr_ref, gw_ref, gbias_ref, qg_ref, kg_ref, bd_ref,
                   q_o, k_o, v_o, qb_o, kb_o, la_o, vb_o, gb_o):
    xn = _rms(x_ref[...], g_ref[...]).astype(BF16)

    def seg(a, b):
        return _mm(xn, w_ref[:, a:b])

    def head_norm(h, gain):
        ss = _mm((h * h).astype(BF16), bd_ref[...]) * (1.0 / HEAD_DIM_A)
        return h * lax.rsqrt(ss + EPS) * gain

    q_o[...] = (head_norm(seg(0, WA), qg_ref[...]) * (HEAD_DIM_A ** -0.5)).astype(BF16)
    k_o[...] = head_norm(seg(WA, 2 * WA), kg_ref[...])
    v_o[...] = seg(2 * WA, 3 * WA)
    o = 3 * WA
    hb = seg(o, o + 2 * KB)
    qb_o[...] = hb[:, :KB] * (HEAD_K_B ** -0.5)
    kb_o[...] = hb[:, KB:]
    vb_o[...] = seg(o + 2 * KB, o + 2 * KB + VB).astype(BF16)
    gb_o[...] = seg(o + 2 * KB + VB, o + 2 * KB + 2 * VB)
    lr = _mm(xn, wlr_ref[...])
    logit = _mm(lr.astype(BF16), gw_ref[...]) + gbias_ref[...]
    log_sig = jnp.minimum(logit, 0.0) - jnp.log1p(jnp.exp(-jnp.abs(logit)))
    la_o[...] = log_sig * (1.0 / GATE_NORM)


def _inproj(x, g, w_main, w_lr, gate_w, gate_b, qg, kg, bd, *, tm=512):
    rows, d = x.shape
    tm = _row_tile(rows, tm)
    row = lambda w: pl.BlockSpec((tm, w), lambda i: (i, 0))
    full = lambda a: pl.BlockSpec(a.shape, lambda i: (0,) * a.ndim)
    outs = [(WA, BF16), (WA, F32), (WA, F32), (KB, F32), (KB, F32), (KB, F32), (VB, BF16), (VB, F32)]
    return pl.pallas_call(
        _inproj_kernel,
        grid=(rows // tm,),
        in_specs=[row(d), full(g), full(w_main), full(w_lr), full(gate_w), full(gate_b),
                  full(qg), full(kg), full(bd)],
        out_specs=[row(w) for w, _ in outs],
        out_shape=[jax.ShapeDtypeStruct((rows, w), dt) for w, dt in outs],
        compiler_params=_params("parallel"),
        name="inproj",
    )(x, g, w_main, w_lr, gate_w, gate_b, qg, kg, bd)


def _softmax_pv(scores, values):
    m = scores[0].max(axis=-1, keepdims=True)
    for s in scores[1:]:
        m = jnp.maximum(m, s.max(axis=-1, keepdims=True))
    l = 0.0
    pv = 0.0
    for s, v in zip(scores, values):
        p = jnp.exp(s - m)
        l = l + p.sum(axis=-1, keepdims=True)
        pv = pv + _mm(p.astype(BF16), v)
    return pv / l


GROUP = 2
GROUP_ROWS = GROUP * CHUNK
WINDOW = PAD + GROUP_ROWS


def _attn_prompt_kernel(q_ref, k_ref, v_ref, bias_ref, o_ref, kp_scr, vp_scr, *, ng):
    t = ng * GROUP_ROWS
    zeros = jnp.zeros((PAD, WA), BF16)
    kp_scr[0:PAD, :] = zeros
    vp_scr[0:PAD, :] = zeros
    kp_scr[PAD:PAD + t, :] = k_ref[...].astype(BF16)
    vp_scr[PAD:PAD + t, :] = v_ref[...].astype(BF16)
    first_half = lax.broadcasted_iota(jnp.int32, (GROUP_ROWS, LANES), 1) < HEAD_DIM_A
    col = lax.broadcasted_iota(jnp.int32, (2 * GROUP_ROWS, WINDOW), 1)

    def group(g, carry, *, ramp):
        r0 = pl.multiple_of(g * GROUP_ROWS, GROUP_ROWS)
        for p in range(N_HEADS_A // 2):
            lanes = slice(p * LANES, (p + 1) * LANES)
            q2 = q_ref[pl.ds(r0, GROUP_ROWS), lanes]
            kw = kp_scr[pl.ds(r0, WINDOW), lanes]
            vw = vp_scr[pl.ds(r0, WINDOW), lanes]
            zero = jnp.zeros_like(q2)
            qs = jnp.concatenate([jnp.where(first_half, q2, zero), jnp.where(first_half, zero, q2)], axis=0)
            s = _mm_nt(qs, kw) + bias_ref[p]
            if ramp:
                s = jnp.where(col >= PAD - g * GROUP_ROWS, s, -1e30)
            o = _softmax_pv([s], [vw])
            o_ref[pl.ds(r0, GROUP_ROWS), lanes] = jnp.where(
                first_half, o[:GROUP_ROWS], o[GROUP_ROWS:]).astype(BF16)
        return carry

    n_ramp = min(ng, PAD // GROUP_ROWS)
    lax.fori_loop(0, n_ramp, functools.partial(group, ramp=True), 0)
    lax.fori_loop(n_ramp, ng, functools.partial(group, ramp=False), 0)


def _attn_prompt(q, k, v, bias, nb, t):
    ng = t // GROUP_ROWS
    blk = pl.BlockSpec((t, WA), lambda b: (b, 0))
    return pl.pallas_call(
        functools.partial(_attn_prompt_kernel, ng=ng),
        grid=(nb,),
        in_specs=[blk, blk, blk, pl.BlockSpec(bias.shape, lambda b: (0, 0, 0))],
        out_specs=blk,
        out_shape=jax.ShapeDtypeStruct((nb * t, WA), BF16),
        scratch_shapes=[pltpu.VMEM((PAD + t, WA), BF16), pltpu.VMEM((PAD + t, WA), BF16)],
        compiler_params=_params("parallel"),
        name="attn_prompt",
    )(q, k, v, bias)


def _attn_sample_kernel(q_ref, kn_ref, vn_ref, ck_ref, cv_ref, bias_c_ref, bias_n_ref, o_ref):
    t = q_ref.shape[0]
    first_half = lax.broadcasted_iota(jnp.int32, (t, LANES), 1) < HEAD_DIM_A
    for p in range(N_HEADS_A // 2):
        lanes = slice(p * LANES, (p + 1) * LANES)
        q2 = q_ref[:, lanes]
        kc = ck_ref[0, :, lanes].astype(BF16)
        vc = cv_ref[0, :, lanes].astype(BF16)
        kn = kn_ref[:, lanes].astype(BF16)
        vn = vn_ref[:, lanes].astype(BF16)
        outs = []
        for e in range(2):
            qm = jnp.where(first_half if e == 0 else ~first_half, q2, jnp.zeros_like(q2))
            s_c = _mm_nt(qm, kc) + bias_c_ref[2 * p + e]
            s_n = _mm_nt(qm, kn) + bias_n_ref[2 * p + e]
            outs.append(_softmax_pv([s_c, s_n], [vc, vn]))
        o_ref[:, lanes] = jnp.where(first_half, outs[0], outs[1]).astype(BF16)


def _attn_sample(q, k_new, v_new, cache_k, cache_v, bias_c, bias_n, nb, t):
    lc = cache_k.shape[1]
    blk = pl.BlockSpec((t, WA), lambda b: (b, 0))
    cblk = pl.BlockSpec((1, lc, WA), lambda b: (b, 0, 0))
    full = lambda a: pl.BlockSpec(a.shape, lambda b: (0,) * a.ndim)
    return pl.pallas_call(
        _attn_sample_kernel,
        grid=(nb,),
        in_specs=[blk, blk, blk, cblk, cblk, full(bias_c), full(bias_n)],
        out_specs=blk,
        out_shape=jax.ShapeDtypeStruct((nb * t, WA), BF16),
        compiler_params=_params("parallel"),
        name="attn_sample",
    )(q, k_new, v_new, cache_k, cache_v, bias_c, bias_n)


def _gla_kernel(q_ref, k_ref, v_ref, la_ref, s0_ref, o_ref, sout_ref, st_scr, *, nc, c_len):
    n_pairs = N_HEADS_B // 2
    for p in range(n_pairs):
        st_scr[p] = s0_ref[0, p * LANES:(p + 1) * LANES, :].T
    ri = lax.broadcasted_iota(jnp.int32, (c_len, c_len), 0)
    ci = lax.broadcasted_iota(jnp.int32, (c_len, c_len), 1)
    causal = ri >= ci
    tril = causal.astype(BF16)
    first_half = lax.broadcasted_iota(jnp.int32, (c_len, LANES), 1) < HEAD_K_B
    first_half_sq = lax.broadcasted_iota(jnp.int32, (LANES, LANES), 1) < HEAD_K_B

    def chunk(c, carry):
        r0 = pl.multiple_of(c * c_len, c_len)
        la = la_ref[pl.ds(r0, c_len), :]
        la_hi = la.astype(BF16)
        la_lo = (la - la_hi.astype(F32)).astype(BF16)
        b = _mm(tril, la_hi) + _mm(tril, la_lo)
        b_last = b[c_len - 1:c_len, :]
        ref = 0.5 * b_last
        q = q_ref[pl.ds(r0, c_len), :]
        k = k_ref[pl.ds(r0, c_len), :]
        qt = (q * jnp.exp(b - ref)).astype(BF16)
        kt = (k * jnp.exp(ref - b)).astype(BF16)
        qd = (q * jnp.exp(b)).astype(BF16)
        kd = (k * jnp.exp(b_last - b)).astype(BF16)
        dl = jnp.exp(b_last)
        for p in range(n_pairs):
            lanes = slice(p * LANES, (p + 1) * LANES)
            st = st_scr[p]
            st_b = st.astype(BF16)
            kt2 = kt[:, lanes]
            kd2 = kd[:, lanes]
            upd = []
            for e in range(2):
                h = 2 * p + e
                hm = first_half if e == 0 else ~first_half
                qt_m = jnp.where(hm, qt[:, lanes], jnp.zeros_like(kt2))
                qd_m = jnp.where(hm, qd[:, lanes], jnp.zeros_like(kt2))
                a = jnp.where(causal, _mm_nt(qt_m, kt2), 0.0).astype(BF16)
                vh = v_ref[pl.ds(r0, c_len), h * HEAD_V_B:(h + 1) * HEAD_V_B]
                o_ref[pl.ds(r0, c_len), h * HEAD_V_B:(h + 1) * HEAD_V_B] = _mm(a, vh) + _mm_nt(qd_m, st_b)
                upd.append(_mm_tn(vh, kd2))
            st_scr[p] = st * dl[:, lanes] + jnp.where(first_half_sq, upd[0], upd[1])
        return carry

    lax.fori_loop(0, nc, chunk, 0)
    for p in range(n_pairs):
        sout_ref[0, p * LANES:(p + 1) * LANES, :] = st_scr[p].T


def _gla(q, k, v, la, s0, nb, t, c_len):
    nc = t // c_len
    kblk = pl.BlockSpec((t, KB), lambda b: (b, 0))
    vblk = pl.BlockSpec((t, VB), lambda b: (b, 0))
    sblk = pl.BlockSpec((1, KB, HEAD_V_B), lambda b: (b, 0, 0))
    return pl.pallas_call(
        functools.partial(_gla_kernel, nc=nc, c_len=c_len),
        grid=(nb,),
        in_specs=[kblk, kblk, vblk, kblk, sblk],
        out_specs=[vblk, sblk],
        out_shape=[jax.ShapeDtypeStruct((nb * t, VB), F32),
                   jax.ShapeDtypeStruct((nb, KB, HEAD_V_B), F32)],
        scratch_shapes=[pltpu.VMEM((N_HEADS_B // 2, LANES, LANES), F32)],
        compiler_params=_params("parallel"),
        name="gla",
    )(q, k, v, la, s0)


def _merge_kernel(x_ref, oa_ref, ob_ref, gb_ref, og_ref, wa_ref, wb_ref, o_ref):
    parts = []
    for h in range(N_HEADS_B):
        lanes = slice(h * HEAD_V_B, (h + 1) * HEAD_V_B)
        gb = gb_ref[:, lanes]
        parts.append((_rms(ob_ref[:, lanes], og_ref[...]) * (gb * _sigmoid(gb))).astype(BF16))
    obn = jnp.concatenate(parts, axis=1)
    o_ref[...] = x_ref[...] + _mm(oa_ref[...], wa_ref[...]) + _mm(obn, wb_ref[...])


def _merge(x, oa, ob, gb, og, w_a, w_b, *, tm=512):
    rows, d = x.shape
    tm = _row_tile(rows, tm)
    row = lambda w: pl.BlockSpec((tm, w), lambda i: (i, 0))
    full = lambda a: pl.BlockSpec(a.shape, lambda i: (0,) * a.ndim)
    return pl.pallas_call(
        _merge_kernel,
        grid=(rows // tm,),
        in_specs=[row(d), row(WA), row(VB), row(VB), full(og), full(w_a), full(w_b)],
        out_specs=row(d),
        out_shape=jax.ShapeDtypeStruct((rows, d), F32),
        compiler_params=_params("parallel"),
        name="merge",
    )(x, oa, ob, gb, og, w_a, w_b)


def _conv_kernel(x_ref, init_ref, g_ref, w1_ref, b1_ref, wdw_ref, bdw_ref, lng_ref, lnb_ref,
                 w2_ref, b2_ref, o_ref, tail_ref, buf_scr, *, tt):
    d = x_ref.shape[1]
    lo = CONV_HALO - (CONV_W - 1)

    @pl.when(pl.program_id(1) == 0)
    def _():
        buf_scr[lo:CONV_HALO, :] = init_ref[0]

    x = x_ref[...]
    u = _mm(_rms(x, g_ref[...]).astype(BF16), w1_ref[...]) + b1_ref[...]
    buf_scr[CONV_HALO:CONV_HALO + tt, :] = u[:, :d] * _sigmoid(u[:, d:])
    y = bdw_ref[...]
    for kk in range(CONV_W):
        y = y + wdw_ref[kk:kk + 1, :] * buf_scr[lo + kk:lo + kk + tt, :]
    mu = jnp.mean(y, axis=-1, keepdims=True)
    yc = y - mu
    var = jnp.mean(yc * yc, axis=-1, keepdims=True)
    yn = yc * lax.rsqrt(var + EPS) * lng_ref[...] + lnb_ref[...]
    z = (yn * _sigmoid(yn)).astype(BF16)
    o_ref[...] = x + _mm(z, w2_ref[...]) + b2_ref[...]
    tail = buf_scr[tt:tt + CONV_HALO, :]
    tail_ref[0] = tail
    buf_scr[0:CONV_HALO, :] = tail


def _conv(x, init, g, w1, b1, wdw, bdw, lng, lnb, w2, b2, nb, t, *, tt=256):
    rows, d = x.shape
    tt = _row_tile(t, tt)
    nt = t // tt
    full = lambda a: pl.BlockSpec(a.shape, lambda b, i: (0,) * a.ndim)
    xblk = pl.BlockSpec((tt, d), lambda b, i: (b * nt + i, 0))
    return pl.pallas_call(
        functools.partial(_conv_kernel, tt=tt),
        grid=(nb, nt),
        in_specs=[xblk, pl.BlockSpec((1, CONV_W - 1, d), lambda b, i: (b, 0, 0)),
                  full(g), full(w1), full(b1), full(wdw), full(bdw), full(lng), full(lnb),
                  full(w2), full(b2)],
        out_specs=[xblk, pl.BlockSpec((1, CONV_HALO, d), lambda b, i: (b, 0, 0))],
        out_shape=[jax.ShapeDtypeStruct((rows, d), F32),
                   jax.ShapeDtypeStruct((nb, CONV_HALO, d), F32)],
        scratch_shapes=[pltpu.VMEM((CONV_HALO + tt, d), F32)],
        compiler_params=_params("parallel", "arbitrary"),
        name="conv",
    )(x, init, g, w1, b1, wdw, bdw, lng, lnb, w2, b2)


def _band_bias(table):
    n_far = PAD + CHUNK - REL_MAX
    ext = jnp.concatenate([jnp.repeat(table[:, -1:], n_far, axis=1), table[:, -2::-1]], axis=1)
    return jnp.stack([ext[:, CHUNK - 1 - i:CHUNK - 1 - i + BAND] for i in range(CHUNK)], axis=1)


def _group_bias(bias):
    per_chunk = [jnp.pad(bias, ((0, 0), (0, 0), (c * CHUNK, (GROUP - 1 - c) * CHUNK)), constant_values=-1e30)
                 for c in range(GROUP)]
    stacked = jnp.stack(per_chunk, axis=1)
    return stacked.reshape(N_HEADS_A // 2, 2 * GROUP_ROWS, WINDOW)


def kernel(x_prompt, x_sample, cache_a_k, cache_a_v, state_b, cache_c, norm_mix_g, norm_ffn_g, w_in_ab, a_q_norm_g, a_k_norm_g, a_rel_bias, b_gate_w_up, b_gate_b, b_out_norm_g, w_out_ab, c_w_pw1, c_b_pw1, c_w_dw, c_b_dw, c_ln_g, c_ln_b, c_w_pw2, c_b_pw2, w_ffn_up, w_ffn_down):
    nbp, tp, d = x_prompt.shape
    nbs, ts, _ = x_sample.shape
    depth = norm_mix_g.shape[0]
    lp = min(PAD, tp)
    lc = cache_a_k.shape[2]
    assert lc == PAD and ts <= CHUNK and tp % CHUNK == 0
    xp = x_prompt.reshape(nbp * tp, d)
    xs = x_sample.reshape(nbs * ts, d)
    row = lambda a: a.reshape(1, -1)

    head_id = np.arange(WA) // HEAD_DIM_A
    bd = jnp.asarray(head_id[:, None] == head_id[None, :], BF16)
    p_main = 3 * WA + 2 * KB + 2 * VB

    akp, avp, aks, avs, sbp, sbs, ccp, ccs = [], [], [], [], [], [], [], []
    for l in range(depth):
        g_mix = row(norm_mix_g[l])
        if l % 2 == 0:
            i = l // 2
            w_in = w_in_ab[i]
            w_main = w_in[:, :p_main].astype(BF16)
            w_lr = jnp.pad(w_in[:, p_main:], ((0, 0), (0, LANES - GATE_RANK))).astype(BF16)
            gate_w = jnp.pad(b_gate_w_up[i], ((0, LANES - GATE_RANK), (0, 0))).astype(BF16)
            proj = (g_mix, w_main, w_lr, gate_w, row(b_gate_b[i]),
                    row(jnp.tile(a_q_norm_g[i], N_HEADS_A)), row(jnp.tile(a_k_norm_g[i], N_HEADS_A)), bd)
            bias = _band_bias(a_rel_bias[i])
            og = row(b_out_norm_g[i])
            w_a = w_out_ab[i][:WA].astype(BF16)
            w_b = w_out_ab[i][WA:].astype(BF16)

            q, k, v, qb, kb, la, vb, gb = _inproj(xp, *proj)
            oa = _attn_prompt(q, k, v, _group_bias(bias), nbp, tp)
            ob, sb = _gla(qb, kb, vb, la, jnp.zeros((nbp, KB, HEAD_V_B), F32), nbp, tp, CHUNK)
            akp.append(k.reshape(nbp, tp, N_HEADS_A, HEAD_DIM_A)[:, tp - lp:])
            avp.append(v.reshape(nbp, tp, N_HEADS_A, HEAD_DIM_A)[:, tp - lp:])
            sbp.append(sb.reshape(nbp, N_HEADS_B, HEAD_K_B, HEAD_V_B))
            xp = _merge(xp, oa, ob, gb, og, w_a, w_b)

            q, k, v, qb, kb, la, vb, gb = _inproj(xs, *proj)
            oa = _attn_sample(q, k, v, cache_a_k[i].reshape(nbs, lc, WA), cache_a_v[i].reshape(nbs, lc, WA),
                              bias[:, :ts, :lc], bias[:, :ts, lc:lc + ts], nbs, ts)
            ob, sb = _gla(qb, kb, vb, la, state_b[i].reshape(nbs, KB, HEAD_V_B), nbs, ts, ts)
            aks.append(k.reshape(nbs, ts, N_HEADS_A, HEAD_DIM_A))
            avs.append(v.reshape(nbs, ts, N_HEADS_A, HEAD_DIM_A))
            sbs.append(sb.reshape(nbs, N_HEADS_B, HEAD_K_B, HEAD_V_B))
            xs = _merge(xs, oa, ob, gb, og, w_a, w_b)
        else:
            j = l // 2
            cw = (g_mix, c_w_pw1[j].astype(BF16), row(c_b_pw1[j]), c_w_dw[j], row(c_b_dw[j]),
                  row(c_ln_g[j]), row(c_ln_b[j]), c_w_pw2[j].astype(BF16), row(c_b_pw2[j]))
            xp, tail = _conv(xp, jnp.zeros((nbp, CONV_W - 1, d), F32), *cw, nbp, tp)
            ccp.append(tail[:, CONV_HALO - (CONV_W - 1):])
            xs, tail = _conv(xs, cache_c[j], *cw, nbs, ts)
            ccs.append(tail[:, CONV_HALO - (CONV_W - 1):])
        w_up = w_ffn_up[l].astype(BF16)
        w_down = w_ffn_down[l].astype(BF16)
        g_ffn = row(norm_ffn_g[l])
        xp = _ffn(xp, g_ffn, w_up, w_down)
        xs = _ffn(xs, g_ffn, w_up, w_down)
    return (xp.reshape(nbp, tp, d), xs.reshape(nbs, ts, d),
            jnp.stack(akp), jnp.stack(avp), jnp.stack(aks), jnp.stack(avs),
            jnp.stack(sbp), jnp.stack(sbs), jnp.stack(ccp), jnp.stack(ccs))
```

```python
import functools

import numpy as np
import jax
import jax.numpy as jnp
from jax import lax
from jax.experimental import pallas as pl
from jax.experimental.pallas import tpu as pltpu

F32 = jnp.float32
BF16 = jnp.bfloat16

EPS = 1e-6
CHUNK = 64
PAST_CHUNKS = 8
PAD = PAST_CHUNKS * CHUNK
BAND = PAD + CHUNK
N_HEADS_A = 8
HEAD_DIM_A = 64
REL_MIN = -(CHUNK - 1)
REL_MAX = 2 * CHUNK
N_HEADS_B = 4
HEAD_K_B = 64
HEAD_V_B = 128
GATE_RANK = 16
GATE_NORM = 16.0
CONV_W = 31
CONV_HALO = 32
WA = N_HEADS_A * HEAD_DIM_A
KB = N_HEADS_B * HEAD_K_B
VB = N_HEADS_B * HEAD_V_B
LANES = 128
SUBLANES = 8
VMEM_LIMIT = 56 * 1024 * 1024


def _mm(a, b):
    return jnp.dot(a, b, preferred_element_type=F32)


def _mm_nt(a, b):
    return lax.dot_general(a, b, (((1,), (1,)), ((), ())), preferred_element_type=F32)


def _mm_tn(a, b):
    return lax.dot_general(a, b, (((0,), (0,)), ((), ())), preferred_element_type=F32)


def _rms(x, g):
    ms = jnp.mean(x * x, axis=-1, keepdims=True)
    return x * lax.rsqrt(ms + EPS) * g


def _sigmoid(x):
    return 1.0 / (1.0 + jnp.exp(-x))


def _params(*sem):
    return pltpu.CompilerParams(dimension_semantics=sem, vmem_limit_bytes=VMEM_LIMIT)


def _row_tile(rows, want):
    tm = min(rows, want)
    assert rows % tm == 0, (rows, tm)
    return tm


def _ffn_steps(x1, g_ref, wu_ref, wd_ref, o_ref, xn_scr):
    @pl.when(pl.program_id(1) == 0)
    def _():
        x = x1()
        o_ref[...] = x
        xn_scr[...] = _rms(x, g_ref[...]).astype(BF16)

    h = jnp.maximum(_mm(xn_scr[...], wu_ref[...]), 0.0)
    o_ref[...] += _mm((h * h).astype(BF16), wd_ref[...])


def _ffn_kernel(x_ref, g_ref, wu_ref, wd_ref, o_ref, xn_scr):
    _ffn_steps(lambda: x_ref[...], g_ref, wu_ref, wd_ref, o_ref, xn_scr)


def _merge_ffn_kernel(x_ref, oa_ref, ob_ref, gb_ref, og_ref, wa_ref, wb_ref,
                      g_ref, wu_ref, wd_ref, o_ref, xn_scr):
    def merged():
        parts = []
        for h in range(N_HEADS_B):
            lanes = slice(h * HEAD_V_B, (h + 1) * HEAD_V_B)
            gb = gb_ref[:, lanes]
            parts.append((_rms(ob_ref[:, lanes], og_ref[...]) * (gb * _sigmoid(gb))).astype(BF16))
        obn = jnp.concatenate(parts, axis=1)
        return x_ref[...] + _mm(oa_ref[...], wa_ref[...]) + _mm(obn, wb_ref[...])

    _ffn_steps(merged, g_ref, wu_ref, wd_ref, o_ref, xn_scr)


def _ffn(x, g, w_up, w_down, merge=None, *, tm=1024, tf=1024):
    rows, d = x.shape
    dff = w_up.shape[1]
    tm = _row_tile(rows, tm)
    row = lambda w: pl.BlockSpec((tm, w), lambda i, j: (i, 0))
    full = lambda a: pl.BlockSpec(a.shape, lambda i, j: (0,) * a.ndim)
    ffn_specs = [full(g), pl.BlockSpec((d, tf), lambda i, j: (0, j)), pl.BlockSpec((tf, d), lambda i, j: (j, 0))]
    if merge is None:
        body, args, specs = _ffn_kernel, (x,), [row(d)]
    else:
        oa, ob, gb, og, w_a, w_b = merge
        body, args = _merge_ffn_kernel, (x, oa, ob, gb, og, w_a, w_b)
        specs = [row(d), row(WA), row(VB), row(VB), full(og), full(w_a), full(w_b)]
    return pl.pallas_call(
        body,
        grid=(rows // tm, dff // tf),
        in_specs=specs + ffn_specs,
        out_specs=row(d),
        out_shape=jax.ShapeDtypeStruct((rows, d), F32),
        scratch_shapes=[pltpu.VMEM((tm, d), BF16)],
        compiler_params=_params("parallel", "arbitrary"),
        name="ffn" if merge is None else "merge_ffn",
    )(*args, g, w_up, w_down)


def _inproj_kernel(x_ref, g_ref, w_ref, wlr_ref, gw_ref, gbias_ref, qg_ref, kg_ref, bd_ref,
                   q_o, k_o, v_o, qb_o, kb_o, la_o, vb_o, gb_o):
    xn = _rms(x_ref[...], g_ref[...]).astype(BF16)

    def seg(a, b):
        return _mm(xn, w_ref[:, a:b])

    def head_norm(h, gain):
        ss = _mm((h * h).astype(BF16), bd_ref[...]) * (1.0 / HEAD_DIM_A)
        return h * lax.rsqrt(ss + EPS) * gain

    q_o[...] = (head_norm(seg(0, WA), qg_ref[...]) * (HEAD_DIM_A ** -0.5)).astype(BF16)
    k_o[...] = head_norm(seg(WA, 2 * WA), kg_ref[...])
    v_o[...] = seg(2 * WA, 3 * WA)
    o = 3 * WA
    hb = seg(o, o + 2 * KB)
    qb_o[...] = hb[:, :KB] * (HEAD_K_B ** -0.5)
    kb_o[...] = hb[:, KB:]
    vb_o[...] = seg(o + 2 * KB, o + 2 * KB + VB).astype(BF16)
    gb_o[...] = seg(o + 2 * KB + VB, o + 2 * KB + 2 * VB)
    lr = _mm(xn, wlr_ref[...])
    logit = _mm(lr.astype(BF16), gw_ref[...]) + gbias_ref[...]
    log_sig = jnp.minimum(logit, 0.0) - jnp.log1p(jnp.exp(-jnp.abs(logit)))
    la_o[...] = log_sig * (1.0 / GATE_NORM)


def _inproj(x, g, w_main, w_lr, gate_w, gate_b, qg, kg, bd, *, tm=512):
    rows, d = x.shape
    tm = _row_tile(rows, tm)
    row = lambda w: pl.BlockSpec((tm, w), lambda i: (i, 0))
    full = lambda a: pl.BlockSpec(a.shape, lambda i: (0,) * a.ndim)
    outs = [(WA, BF16), (WA, F32), (WA, F32), (KB, F32), (KB, F32), (KB, F32), (VB, BF16), (VB, F32)]
    return pl.pallas_call(
        _inproj_kernel,
        grid=(rows // tm,),
        in_specs=[row(d), full(g), full(w_main), full(w_lr), full(gate_w), full(gate_b),
                  full(qg), full(kg), full(bd)],
        out_specs=[row(w) for w, _ in outs],
        out_shape=[jax.ShapeDtypeStruct((rows, w), dt) for w, dt in outs],
        compiler_params=_params("parallel"),
        name="inproj",
    )(x, g, w_main, w_lr, gate_w, gate_b, qg, kg, bd)


def _softmax_pv(scores, values):
    m = scores[0].max(axis=-1, keepdims=True)
    for s in scores[1:]:
        m = jnp.maximum(m, s.max(axis=-1, keepdims=True))
    l = 0.0
    pv = 0.0
    for s, v in zip(scores, values):
        p = jnp.exp(s - m)
        l = l + p.sum(axis=-1, keepdims=True)
        pv = pv + _mm(p.astype(BF16), v)
    return pv / l


GROUP = 2
GROUP_ROWS = GROUP * CHUNK
WINDOW = PAD + GROUP_ROWS


def _attn_prompt_kernel(q_ref, k_ref, v_ref, bias_ref, o_ref, kp_scr, vp_scr, *, ng):
    t = ng * GROUP_ROWS
    zeros = jnp.zeros((PAD, WA), BF16)
    kp_scr[0:PAD, :] = zeros
    vp_scr[0:PAD, :] = zeros
    kp_scr[PAD:PAD + t, :] = k_ref[...].astype(BF16)
    vp_scr[PAD:PAD + t, :] = v_ref[...].astype(BF16)
    first_half = lax.broadcasted_iota(jnp.int32, (GROUP_ROWS, LANES), 1) < HEAD_DIM_A
    col = lax.broadcasted_iota(jnp.int32, (2 * GROUP_ROWS, WINDOW), 1)

    def group(g, carry, *, ramp):
        r0 = pl.multiple_of(g * GROUP_ROWS, GROUP_ROWS)
        for p in range(N_HEADS_A // 2):
            lanes = slice(p * LANES, (p + 1) * LANES)
            q2 = q_ref[pl.ds(r0, GROUP_ROWS), lanes]
            kw = kp_scr[pl.ds(r0, WINDOW), lanes]
            vw = vp_scr[pl.ds(r0, WINDOW), lanes]
            zero = jnp.zeros_like(q2)
            qs = jnp.concatenate([jnp.where(first_half, q2, zero), jnp.where(first_half, zero, q2)], axis=0)
            s = _mm_nt(qs, kw) + bias_ref[p]
            if ramp:
                s = jnp.where(col >= PAD - g * GROUP_ROWS, s, -1e30)
            o = _softmax_pv([s], [vw])
            o_ref[pl.ds(r0, GROUP_ROWS), lanes] = jnp.where(
                first_half, o[:GROUP_ROWS], o[GROUP_ROWS:]).astype(BF16)
        return carry

    n_ramp = min(ng, PAD // GROUP_ROWS)
    lax.fori_loop(0, n_ramp, functools.partial(group, ramp=True), 0)
    lax.fori_loop(n_ramp, ng, functools.partial(group, ramp=False), 0)


def _attn_prompt(q, k, v, bias, nb, t):
    ng = t // GROUP_ROWS
    blk = pl.BlockSpec((t, WA), lambda b: (b, 0))
    return pl.pallas_call(
        functools.partial(_attn_prompt_kernel, ng=ng),
        grid=(nb,),
        in_specs=[blk, blk, blk, pl.BlockSpec(bias.shape, lambda b: (0, 0, 0))],
        out_specs=blk,
        out_shape=jax.ShapeDtypeStruct((nb * t, WA), BF16),
        scratch_shapes=[pltpu.VMEM((PAD + t, WA), BF16), pltpu.VMEM((PAD + t, WA), BF16)],
        compiler_params=_params("parallel"),
        name="attn_prompt",
    )(q, k, v, bias)


def _attn_sample_kernel(q_ref, kn_ref, vn_ref, ck_ref, cv_ref, bias_c_ref, bias_n_ref, o_ref, *, ns, t):
    first_half = lax.broadcasted_iota(jnp.int32, (t, LANES), 1) < HEAD_DIM_A
    for s in range(ns):
        rows = slice(s * t, (s + 1) * t)
        for p in range(N_HEADS_A // 2):
            lanes = slice(p * LANES, (p + 1) * LANES)
            q2 = q_ref[rows, lanes]
            zero = jnp.zeros_like(q2)
            qs = jnp.concatenate([jnp.where(first_half, q2, zero), jnp.where(first_half, zero, q2)], axis=0)
            s_c = _mm_nt(qs, ck_ref[s, :, lanes].astype(BF16)) + bias_c_ref[p]
            s_n = _mm_nt(qs, kn_ref[rows, lanes].astype(BF16)) + bias_n_ref[p]
            o = _softmax_pv([s_c, s_n], [cv_ref[s, :, lanes].astype(BF16), vn_ref[rows, lanes].astype(BF16)])
            o_ref[rows, lanes] = jnp.where(first_half, o[:t], o[t:]).astype(BF16)


def _attn_sample(q, k_new, v_new, cache_k, cache_v, bias_c, bias_n, nb, t, *, ns=4):
    lc = cache_k.shape[1]
    assert nb % ns == 0
    blk = pl.BlockSpec((ns * t, WA), lambda b: (b, 0))
    cblk = pl.BlockSpec((ns, lc, WA), lambda b: (b, 0, 0))
    full = lambda a: pl.BlockSpec(a.shape, lambda b: (0,) * a.ndim)
    return pl.pallas_call(
        functools.partial(_attn_sample_kernel, ns=ns, t=t),
        grid=(nb // ns,),
        in_specs=[blk, blk, blk, cblk, cblk, full(bias_c), full(bias_n)],
        out_specs=blk,
        out_shape=jax.ShapeDtypeStruct((nb * t, WA), BF16),
        compiler_params=_params("parallel"),
        name="attn_sample",
    )(q, k_new, v_new, cache_k, cache_v, bias_c, bias_n)


def _gla_kernel(q_ref, k_ref, v_ref, la_ref, s0_ref, o_ref, sout_ref, st_scr, *, ns, nblk, ncb, c_len):
    n_pairs = N_HEADS_B // 2
    r_len = ncb * c_len

    @pl.when(pl.program_id(1) == 0)
    def _():
        for s in range(ns):
            for p in range(n_pairs):
                st_scr[s, p] = s0_ref[s, p * LANES:(p + 1) * LANES, :].T

    ri = lax.broadcasted_iota(jnp.int32, (r_len, r_len), 0)
    ci = lax.broadcasted_iota(jnp.int32, (r_len, r_len), 1)
    same_chunk = (ri // c_len) == (ci // c_len)
    causal = same_chunk & (ri >= ci)
    tril = causal.astype(BF16)
    ones_bd = same_chunk.astype(BF16)
    row_chunk = lax.broadcasted_iota(jnp.int32, (r_len, HEAD_V_B), 0) // c_len
    first_half = lax.broadcasted_iota(jnp.int32, (r_len, LANES), 1) < HEAD_K_B
    first_half_sq = lax.broadcasted_iota(jnp.int32, (LANES, LANES), 1) < HEAD_K_B

    def one_block(s, r0):
        rows = pl.ds(r0, r_len)
        la = la_ref[s, rows, :]
        la_hi = la.astype(BF16)
        la_lo = (la - la_hi.astype(F32)).astype(BF16)
        b = _mm(tril, la_hi) + _mm(tril, la_lo)
        bl = _mm(ones_bd, la_hi) + _mm(ones_bd, la_lo)
        ref = 0.5 * bl
        q = q_ref[s, rows, :]
        k = k_ref[s, rows, :]
        qt = (q * jnp.exp(b - ref)).astype(BF16)
        kt = (k * jnp.exp(ref - b)).astype(BF16)
        qd = (q * jnp.exp(b)).astype(BF16)
        kd = (k * jnp.exp(bl - b)).astype(BF16)
        dl = jnp.exp(bl)
        for p in range(n_pairs):
            lanes = slice(p * LANES, (p + 1) * LANES)
            kt2 = kt[:, lanes]
            kd2 = kd[:, lanes]
            zero = jnp.zeros_like(kt2)
            vs, qds, intra, upd = [], [], [], []
            for e in range(2):
                hm = first_half if e == 0 else ~first_half
                vh = v_ref[s, rows, (2 * p + e) * HEAD_V_B:(2 * p + e + 1) * HEAD_V_B]
                a = jnp.where(causal, _mm_nt(jnp.where(hm, qt[:, lanes], zero), kt2), 0.0).astype(BF16)
                intra.append(_mm(a, vh))
                qds.append(jnp.where(hm, qd[:, lanes], zero))
                v_bd = jnp.concatenate([jnp.where(row_chunk == c, vh, jnp.zeros_like(vh)) for c in range(ncb)], axis=1)
                upd.append(_mm_tn(v_bd, kd2))
            st = st_scr[s, p]
            inter = [[], []]
            for c in range(ncb):
                crow = slice(c * c_len, (c + 1) * c_len)
                q_stack = jnp.concatenate([qds[0][crow], qds[1][crow]], axis=0)
                o_c = _mm_nt(q_stack, st.astype(BF16))
                inter[0].append(o_c[:c_len])
                inter[1].append(o_c[c_len:])
                urow = slice(c * HEAD_V_B, (c + 1) * HEAD_V_B)
                st = st * dl[c * c_len:c * c_len + 1, lanes] + jnp.where(first_half_sq, upd[0][urow], upd[1][urow])
            st_scr[s, p] = st
            for e in range(2):
                vlanes = slice((2 * p + e) * HEAD_V_B, (2 * p + e + 1) * HEAD_V_B)
                o_inter = inter[e][0] if ncb == 1 else jnp.concatenate(inter[e], axis=0)
                o_ref[s, rows, vlanes] = intra[e] + o_inter

    def block(j, carry):
        for s in range(ns):
            one_block(s, pl.multiple_of(j * r_len, r_len))
        return carry

    lax.fori_loop(0, nblk, block, 0)

    @pl.when(pl.program_id(1) == pl.num_programs(1) - 1)
    def _():
        for s in range(ns):
            for p in range(n_pairs):
                sout_ref[s, p * LANES:(p + 1) * LANES, :] = st_scr[s, p].T


def _gla(q, k, v, la, s0, nb, t, c_len, *, ns, tb=512, r_len=256):
    tb = _row_tile(t, tb)
    r_len = _row_tile(tb, r_len)
    assert nb % ns == 0 and r_len % c_len == 0
    kblk = pl.BlockSpec((ns, tb, KB), lambda b, j: (b, j, 0))
    vblk = pl.BlockSpec((ns, tb, VB), lambda b, j: (b, j, 0))
    sblk = pl.BlockSpec((ns, KB, HEAD_V_B), lambda b, j: (b, 0, 0))
    per_stream = lambda a: a.reshape(nb, t, a.shape[-1])
    o, s_out = pl.pallas_call(
        functools.partial(_gla_kernel, ns=ns, nblk=tb // r_len, ncb=r_len // c_len, c_len=c_len),
        grid=(nb // ns, t // tb),
        in_specs=[kblk, kblk, vblk, kblk, sblk],
        out_specs=[vblk, sblk],
        out_shape=[jax.ShapeDtypeStruct((nb, t, VB), F32),
                   jax.ShapeDtypeStruct((nb, KB, HEAD_V_B), F32)],
        scratch_shapes=[pltpu.VMEM((ns, N_HEADS_B // 2, LANES, LANES), F32)],
        compiler_params=_params("parallel", "arbitrary"),
        name="gla",
    )(per_stream(q), per_stream(k), per_stream(v), per_stream(la), s0)
    return o.reshape(nb * t, VB), s_out


def _conv_kernel(x_ref, init_ref, g_ref, w1_ref, b1_ref, wdw_ref, bdw_ref, lng_ref, lnb_ref,
                 w2_ref, b2_ref, o_ref, tail_ref, buf_scr, *, ns, tt):
    d = x_ref.shape[1]
    lo = CONV_HALO - (CONV_W - 1)

    @pl.when(pl.program_id(1) == 0)
    def _():
        for s in range(ns):
            buf_scr[s, lo:CONV_HALO, :] = init_ref[s]
            buf_scr[s, CONV_HALO + tt:CONV_HALO + tt + SUBLANES, :] = jnp.zeros((SUBLANES, d), F32)

    x = x_ref[...]
    u = _mm(_rms(x, g_ref[...]).astype(BF16), w1_ref[...]) + b1_ref[...]
    glu = u[:, :d] * _sigmoid(u[:, d:])
    ys = []
    for s in range(ns):
        buf_scr[s, CONV_HALO:CONV_HALO + tt, :] = glu[s * tt:(s + 1) * tt]
        y = bdw_ref[...]
        for r in range(SUBLANES):
            part = None
            for k in range(lo, lo + CONV_W):
                if k % SUBLANES == r:
                    term = wdw_ref[k - lo:k - lo + 1, :] * buf_scr[s, k - r:k - r + tt + SUBLANES, :]
                    part = term if part is None else part + term
            y = y + part[r:r + tt, :]
        ys.append(y)
        tail = buf_scr[s, tt:tt + CONV_HALO, :]
        tail_ref[s] = tail
        buf_scr[s, 0:CONV_HALO, :] = tail
    y = ys[0] if ns == 1 else jnp.concatenate(ys, axis=0)
    mu = jnp.mean(y, axis=-1, keepdims=True)
    yc = y - mu
    var = jnp.mean(yc * yc, axis=-1, keepdims=True)
    yn = yc * lax.rsqrt(var + EPS) * lng_ref[...] + lnb_ref[...]
    z = (yn * _sigmoid(yn)).astype(BF16)
    o_ref[...] = x + _mm(z, w2_ref[...]) + b2_ref[...]


def _conv(x, init, g, w1, b1, wdw, bdw, lng, lnb, w2, b2, nb, t, *, tt=256, ns=8):
    rows, d = x.shape
    tt = _row_tile(t, tt)
    nt = t // tt
    ns = ns if nt == 1 else 1
    assert nb % ns == 0
    full = lambda a: pl.BlockSpec(a.shape, lambda b, i: (0,) * a.ndim)
    xblk = pl.BlockSpec((ns * tt, d), lambda b, i: (b * nt + i, 0))
    return pl.pallas_call(
        functools.partial(_conv_kernel, ns=ns, tt=tt),
        grid=(nb // ns, nt),
        in_specs=[xblk, pl.BlockSpec((ns, CONV_W - 1, d), lambda b, i: (b, 0, 0)),
                  full(g), full(w1), full(b1), full(wdw), full(bdw), full(lng), full(lnb),
                  full(w2), full(b2)],
        out_specs=[xblk, pl.BlockSpec((ns, CONV_HALO, d), lambda b, i: (b, 0, 0))],
        out_shape=[jax.ShapeDtypeStruct((rows, d), F32),
                   jax.ShapeDtypeStruct((nb, CONV_HALO, d), F32)],
        scratch_shapes=[pltpu.VMEM((ns, CONV_HALO + tt + SUBLANES, d), F32)],
        compiler_params=_params("parallel", "arbitrary"),
        name="conv",
    )(x, init, g, w1, b1, wdw, bdw, lng, lnb, w2, b2)


def _band_bias(table):
    n_far = PAD + CHUNK - REL_MAX
    ext = jnp.concatenate([jnp.repeat(table[:, -1:], n_far, axis=1), table[:, -2::-1]], axis=1)
    period = ext.shape[1] + 1
    ext = jnp.pad(ext, ((0, 0), (0, 1)))
    wrapped = jnp.tile(ext, (1, CHUNK + 1))[:, :CHUNK * (period + 1)].reshape(-1, CHUNK, period + 1)
    return wrapped[:, ::-1, :BAND]


def _group_bias(bias):
    per_chunk = [jnp.pad(bias, ((0, 0), (0, 0), (c * CHUNK, (GROUP - 1 - c) * CHUNK)), constant_values=-1e30)
                 for c in range(GROUP)]
    stacked = jnp.stack(per_chunk, axis=1)
    return stacked.reshape(N_HEADS_A // 2, 2 * GROUP_ROWS, WINDOW)


def kernel(x_prompt, x_sample, cache_a_k, cache_a_v, state_b, cache_c, norm_mix_g, norm_ffn_g, w_in_ab, a_q_norm_g, a_k_norm_g, a_rel_bias, b_gate_w_up, b_gate_b, b_out_norm_g, w_out_ab, c_w_pw1, c_b_pw1, c_w_dw, c_b_dw, c_ln_g, c_ln_b, c_w_pw2, c_b_pw2, w_ffn_up, w_ffn_down):
    nbp, tp, d = x_prompt.shape
    nbs, ts, _ = x_sample.shape
    depth = norm_mix_g.shape[0]
    lp = min(PAD, tp)
    lc = cache_a_k.shape[2]
    assert lc == PAD and ts <= CHUNK and tp % CHUNK == 0
    xp = x_prompt.reshape(nbp * tp, d)
    xs = x_sample.reshape(nbs * ts, d)
    row = lambda a: a.reshape(1, -1)

    head_id = np.arange(WA) // HEAD_DIM_A
    bd = jnp.asarray(head_id[:, None] == head_id[None, :], BF16)
    p_main = 3 * WA + 2 * KB + 2 * VB

    akp, avp, aks, avs, sbp, sbs, ccp, ccs = [], [], [], [], [], [], [], []
    for l in range(depth):
        g_mix = row(norm_mix_g[l])
        if l % 2 == 0:
            i = l // 2
            w_in = w_in_ab[i]
            w_main = w_in[:, :p_main].astype(BF16)
            w_lr = jnp.pad(w_in[:, p_main:], ((0, 0), (0, LANES - GATE_RANK))).astype(BF16)
            gate_w = jnp.pad(b_gate_w_up[i], ((0, LANES - GATE_RANK), (0, 0))).astype(BF16)
            proj = (g_mix, w_main, w_lr, gate_w, row(b_gate_b[i]),
                    row(jnp.tile(a_q_norm_g[i], N_HEADS_A)), row(jnp.tile(a_k_norm_g[i], N_HEADS_A)), bd)
            bias = _band_bias(a_rel_bias[i])
            og = row(b_out_norm_g[i])
            w_a = w_out_ab[i][:WA].astype(BF16)
            w_b = w_out_ab[i][WA:].astype(BF16)

            q, k, v, qb, kb, la, vb, gb = _inproj(xp, *proj)
            oa = _attn_prompt(q, k, v, _group_bias(bias), nbp, tp)
            ob, sb = _gla(qb, kb, vb, la, jnp.zeros((nbp, KB, HEAD_V_B), F32), nbp, tp, CHUNK, ns=min(nbp, 2))
            akp.append(k.reshape(nbp, tp, N_HEADS_A, HEAD_DIM_A)[:, tp - lp:])
            avp.append(v.reshape(nbp, tp, N_HEADS_A, HEAD_DIM_A)[:, tp - lp:])
            sbp.append(sb.reshape(nbp, N_HEADS_B, HEAD_K_B, HEAD_V_B))
            merge_p = (oa, ob, gb, og, w_a, w_b)

            q, k, v, qb, kb, la, vb, gb = _inproj(xs, *proj)
            pair_rows = lambda a: a.reshape(N_HEADS_A // 2, 2 * ts, a.shape[-1])
            oa = _attn_sample(q, k, v, cache_a_k[i].reshape(nbs, lc, WA), cache_a_v[i].reshape(nbs, lc, WA),
                              pair_rows(bias[:, :ts, :lc]), pair_rows(bias[:, :ts, lc:lc + ts]), nbs, ts,
                              ns=min(nbs, 4))
            ob, sb = _gla(qb, kb, vb, la, state_b[i].reshape(nbs, KB, HEAD_V_B), nbs, ts, ts, ns=min(nbs, 8))
            aks.append(k.reshape(nbs, ts, N_HEADS_A, HEAD_DIM_A))
            avs.append(v.reshape(nbs, ts, N_HEADS_A, HEAD_DIM_A))
            sbs.append(sb.reshape(nbs, N_HEADS_B, HEAD_K_B, HEAD_V_B))
            merge_s = (oa, ob, gb, og, w_a, w_b)
        else:
            merge_p = merge_s = None
            j = l // 2
            cw = (g_mix, c_w_pw1[j].astype(BF16), row(c_b_pw1[j]), c_w_dw[j], row(c_b_dw[j]),
                  row(c_ln_g[j]), row(c_ln_b[j]), c_w_pw2[j].astype(BF16), row(c_b_pw2[j]))
            xp, tail = _conv(xp, jnp.zeros((nbp, CONV_W - 1, d), F32), *cw, nbp, tp)
            ccp.append(tail[:, CONV_HALO - (CONV_W - 1):])
            xs, tail = _conv(xs, cache_c[j], *cw, nbs, ts, ns=min(nbs, 8))
            ccs.append(tail[:, CONV_HALO - (CONV_W - 1):])
        w_up = w_ffn_up[l].astype(BF16)
        w_down = w_ffn_down[l].astype(BF16)
        g_ffn = row(norm_ffn_g[l])
        xp = _ffn(xp, g_ffn, w_up, w_down, merge_p)
        xs = _ffn(xs, g_ffn, w_up, w_down, merge_s)
    return (xp.reshape(nbp, tp, d), xs.reshape(nbs, ts, d),
            jnp.stack(akp), jnp.stack(avp), jnp.stack(aks), jnp.stack(avs),
            jnp.stack(sbp), jnp.stack(sbs), jnp.stack(ccp), jnp.stack(ccs))
```

```python
import functools

import numpy as np
import jax
import jax.numpy as jnp
from jax import lax
from jax.experimental import pallas as pl
from jax.experimental.pallas import tpu as pltpu

F32 = jnp.float32
BF16 = jnp.bfloat16

EPS = 1e-6
CHUNK = 64
PAST_CHUNKS = 8
PAD = PAST_CHUNKS * CHUNK
BAND = PAD + CHUNK
N_HEADS_A = 8
HEAD_DIM_A = 64
REL_MIN = -(CHUNK - 1)
REL_MAX = 2 * CHUNK
N_HEADS_B = 4
HEAD_K_B = 64
HEAD_V_B = 128
GATE_RANK = 16
GATE_NORM = 16.0
CONV_W = 31
CONV_HALO = 32
WA = N_HEADS_A * HEAD_DIM_A
KB = N_HEADS_B * HEAD_K_B
VB = N_HEADS_B * HEAD_V_B
P_MAIN = 3 * WA + 2 * KB + 2 * VB
LANES = 128
SUBLANES = 8
VMEM_LIMIT = 56 * 1024 * 1024


def _mm(a, b):
    return jnp.dot(a, b, preferred_element_type=F32)


def _mm_nt(a, b):
    return lax.dot_general(a, b, (((1,), (1,)), ((), ())), preferred_element_type=F32)


def _mm_tn(a, b):
    return lax.dot_general(a, b, (((0,), (0,)), ((), ())), preferred_element_type=F32)


def _rms(x, g):
    ms = jnp.mean(x * x, axis=-1, keepdims=True)
    return x * lax.rsqrt(ms + EPS) * g


def _sigmoid(x):
    return 1.0 / (1.0 + jnp.exp(-x))


def _params(*sem):
    return pltpu.CompilerParams(dimension_semantics=sem, vmem_limit_bytes=VMEM_LIMIT)


def _row_tile(rows, want):
    tm = min(rows, want)
    assert rows % tm == 0, (rows, tm)
    return tm


def _layer_spec(a, l):
    return pl.BlockSpec((1,) + a.shape[1:], lambda *_: (l,) + (0,) * (a.ndim - 1))


def _full_spec(a):
    return pl.BlockSpec(a.shape, lambda *_: (0,) * a.ndim)


def _ffn_steps(x1, g_ref, wu_ref, wd_ref, o_ref, xn_scr):
    @pl.when(pl.program_id(1) == 0)
    def _():
        x = x1()
        o_ref[...] = x
        xn_scr[...] = _rms(x, g_ref[0]).astype(BF16)

    h = jnp.maximum(_mm(xn_scr[...], wu_ref[0]), 0.0)
    o_ref[...] += _mm((h * h).astype(BF16), wd_ref[0])


def _ffn_kernel(x_ref, g_ref, wu_ref, wd_ref, o_ref, xn_scr):
    _ffn_steps(lambda: x_ref[...], g_ref, wu_ref, wd_ref, o_ref, xn_scr)


def _merge_ffn_kernel(x_ref, oa_ref, ob_ref, gb_ref, og_ref, wa_ref, wb_ref,
                      g_ref, wu_ref, wd_ref, o_ref, xn_scr):
    def merged():
        parts = []
        for h in range(N_HEADS_B):
            lanes = slice(h * HEAD_V_B, (h + 1) * HEAD_V_B)
            gb = gb_ref[:, lanes]
            parts.append((_rms(ob_ref[:, lanes], og_ref[0]) * (gb * _sigmoid(gb))).astype(BF16))
        obn = jnp.concatenate(parts, axis=1)
        return x_ref[...] + _mm(oa_ref[...], wa_ref[0]) + _mm(obn, wb_ref[0])

    _ffn_steps(merged, g_ref, wu_ref, wd_ref, o_ref, xn_scr)


def _ffn(x, g, w_up, w_down, l, merge=None, *, tm=1024, tf=1024):
    rows, d = x.shape
    dff = w_up.shape[2]
    tm = _row_tile(rows, tm)
    row = lambda w: pl.BlockSpec((tm, w), lambda i, j: (i, 0))
    ffn_specs = [_layer_spec(g, l), pl.BlockSpec((1, d, tf), lambda i, j: (l, 0, j)),
                 pl.BlockSpec((1, tf, d), lambda i, j: (l, j, 0))]
    if merge is None:
        body, args, specs = _ffn_kernel, (x,), [row(d)]
    else:
        oa, ob, gb, og, w_out, li = merge
        body, args = _merge_ffn_kernel, (x, oa, ob, gb, og, w_out, w_out)
        specs = [row(d), row(WA), row(VB), row(VB), _layer_spec(og, li),
                 pl.BlockSpec((1, WA, d), lambda i, j: (li, 0, 0)),
                 pl.BlockSpec((1, VB, d), lambda i, j: (li, WA // VB, 0))]
    return pl.pallas_call(
        body,
        grid=(rows // tm, dff // tf),
        in_specs=specs + ffn_specs,
        out_specs=row(d),
        out_shape=jax.ShapeDtypeStruct((rows, d), F32),
        scratch_shapes=[pltpu.VMEM((tm, d), BF16)],
        compiler_params=_params("parallel", "arbitrary"),
        name="ffn" if merge is None else "merge_ffn",
    )(*args, g, w_up, w_down)


def _inproj_kernel(x_ref, g_ref, w_ref, wlr_ref, gw_ref, gbias_ref, qg_ref, kg_ref, bd_ref, *outs,
                   tail_every):
    if tail_every:
        q_o, k_o, v_o, kt_o, vt_o, qb_o, kb_o, la_o, vb_o, gb_o = outs
    else:
        q_o, k_o, v_o, qb_o, kb_o, la_o, vb_o, gb_o = outs
    xn = _rms(x_ref[...], g_ref[0]).astype(BF16)

    def seg(a, b):
        return _mm(xn, w_ref[0, :, a:b])

    def head_norm(h, gain):
        ss = _mm((h * h).astype(BF16), bd_ref[...]) * (1.0 / HEAD_DIM_A)
        return h * lax.rsqrt(ss + EPS) * gain

    q_o[...] = (head_norm(seg(0, WA), qg_ref[0]) * (HEAD_DIM_A ** -0.5)).astype(BF16)
    k = head_norm(seg(WA, 2 * WA), kg_ref[0])
    v = seg(2 * WA, 3 * WA)
    k_o[...] = k.astype(k_o.dtype)
    v_o[...] = v.astype(v_o.dtype)
    if tail_every:
        @pl.when(pl.program_id(0) % tail_every == tail_every - 1)
        def _():
            kt_o[0] = k.T
            vt_o[0] = v.T
    o = 3 * WA
    hb = seg(o, o + 2 * KB)
    qb_o[...] = hb[:, :KB] * (HEAD_K_B ** -0.5)
    kb_o[...] = hb[:, KB:]
    vb_o[...] = seg(o + 2 * KB, o + 2 * KB + VB).astype(BF16)
    gb_o[...] = seg(o + 2 * KB + VB, P_MAIN)
    lr = _mm(xn, wlr_ref[0])
    logit = _mm(lr.astype(BF16), gw_ref[0]) + gbias_ref[0]
    log_sig = jnp.minimum(logit, 0.0) - jnp.log1p(jnp.exp(-jnp.abs(logit)))
    la_o[...] = log_sig * (1.0 / GATE_NORM)


def _inproj(x, g, l, w_in, w_lr, gate_w, gate_b, qg, kg, i, bd, *, stream_len=None, tm=512):
    rows, d = x.shape
    tm = _row_tile(rows, tm)
    row = lambda w: pl.BlockSpec((tm, w), lambda r: (r, 0))
    kv_dtype = BF16 if stream_len else F32
    outs = [(WA, BF16), (WA, kv_dtype), (WA, kv_dtype), (KB, F32), (KB, F32), (KB, F32), (VB, BF16), (VB, F32)]
    out_specs = [row(w) for w, _ in outs]
    out_shape = [jax.ShapeDtypeStruct((rows, w), dt) for w, dt in outs]
    tail_every = None
    if stream_len:
        assert stream_len % tm == 0 and tm == PAD
        tail_every = stream_len // tm
        tail_spec = pl.BlockSpec((1, WA, tm), lambda r: (r // tail_every, 0, 0))
        tail_shape = jax.ShapeDtypeStruct((rows // stream_len, WA, tm), F32)
        out_specs[3:3] = [tail_spec, tail_spec]
        out_shape[3:3] = [tail_shape, tail_shape]
    return pl.pallas_call(
        functools.partial(_inproj_kernel, tail_every=tail_every),
        grid=(rows // tm,),
        in_specs=[row(d), _layer_spec(g, l), _layer_spec(w_in, i), _layer_spec(w_lr, i),
                  _layer_spec(gate_w, i), _layer_spec(gate_b, i), _layer_spec(qg, i),
                  _layer_spec(kg, i), _full_spec(bd)],
        out_specs=out_specs,
        out_shape=out_shape,
        compiler_params=_params("arbitrary"),
        name="inproj",
    )(x, g, w_in, w_lr, gate_w, gate_b, qg, kg, bd)


def _softmax_pv(scores, pv_fns):
    m = scores[0].max(axis=-1, keepdims=True)
    for s in scores[1:]:
        m = jnp.maximum(m, s.max(axis=-1, keepdims=True))
    l = 0.0
    pv = 0.0
    for s, fn in zip(scores, pv_fns):
        p = jnp.exp(s - m)
        l = l + p.sum(axis=-1, keepdims=True)
        pv = pv + fn(p.astype(BF16))
    return pv / l


def _stack_pair(q2, first_half):
    zero = jnp.zeros_like(q2)
    return jnp.concatenate([jnp.where(first_half, q2, zero), jnp.where(first_half, zero, q2)], axis=0)


GROUP = 2
GROUP_ROWS = GROUP * CHUNK
WINDOW = PAD + GROUP_ROWS


def _attn_prompt_kernel(q_ref, k_ref, v_ref, bias_ref, o_ref, kp_scr, vp_scr, *, ng):
    t = ng * GROUP_ROWS
    zeros = jnp.zeros((PAD, WA), BF16)
    kp_scr[0:PAD, :] = zeros
    vp_scr[0:PAD, :] = zeros
    kp_scr[PAD:PAD + t, :] = k_ref[...]
    vp_scr[PAD:PAD + t, :] = v_ref[...]
    first_half = lax.broadcasted_iota(jnp.int32, (GROUP_ROWS, LANES), 1) < HEAD_DIM_A
    col = lax.broadcasted_iota(jnp.int32, (2 * GROUP_ROWS, WINDOW), 1)

    def group(g, carry, *, ramp):
        r0 = pl.multiple_of(g * GROUP_ROWS, GROUP_ROWS)
        for p in range(N_HEADS_A // 2):
            lanes = slice(p * LANES, (p + 1) * LANES)
            kw = kp_scr[pl.ds(r0, WINDOW), lanes]
            vw = vp_scr[pl.ds(r0, WINDOW), lanes]
            s = _mm_nt(_stack_pair(q_ref[pl.ds(r0, GROUP_ROWS), lanes], first_half), kw) + bias_ref[p]
            if ramp:
                s = jnp.where(col >= PAD - g * GROUP_ROWS, s, -1e30)
            o = _softmax_pv([s], [lambda pb: _mm(pb, vw)])
            o_ref[pl.ds(r0, GROUP_ROWS), lanes] = jnp.where(
                first_half, o[:GROUP_ROWS], o[GROUP_ROWS:]).astype(BF16)
        return carry

    n_ramp = min(ng, PAD // GROUP_ROWS)
    lax.fori_loop(0, n_ramp, functools.partial(group, ramp=True), 0)
    lax.fori_loop(n_ramp, ng, functools.partial(group, ramp=False), 0)


def _attn_prompt(q, k, v, bias, nb, t):
    ng = t // GROUP_ROWS
    blk = pl.BlockSpec((t, WA), lambda b: (b, 0))
    return pl.pallas_call(
        functools.partial(_attn_prompt_kernel, ng=ng),
        grid=(nb,),
        in_specs=[blk, blk, blk, _full_spec(bias)],
        out_specs=blk,
        out_shape=jax.ShapeDtypeStruct((nb * t, WA), BF16),
        scratch_shapes=[pltpu.VMEM((PAD + t, WA), BF16), pltpu.VMEM((PAD + t, WA), BF16)],
        compiler_params=_params("parallel"),
        name="attn_prompt",
    )(q, k, v, bias)


def _attn_sample_kernel(q_ref, kn_ref, vn_ref, ckt_ref, cvt_ref, bias_c_ref, bias_n_ref, o_ref, *, ns, t):
    first_half = lax.broadcasted_iota(jnp.int32, (t, LANES), 1) < HEAD_DIM_A
    for s in range(ns):
        rows = slice(s * t, (s + 1) * t)
        for p in range(N_HEADS_A // 2):
            lanes = slice(p * LANES, (p + 1) * LANES)
            qs = _stack_pair(q_ref[rows, lanes], first_half)
            kct = ckt_ref[0, s, lanes, :].astype(BF16)
            vct = cvt_ref[0, s, lanes, :].astype(BF16)
            kn = kn_ref[rows, lanes].astype(BF16)
            vn = vn_ref[rows, lanes].astype(BF16)
            s_c = _mm(qs, kct) + bias_c_ref[p]
            s_n = _mm_nt(qs, kn) + bias_n_ref[p]
            o = _softmax_pv([s_c, s_n], [lambda pb: _mm_nt(pb, vct), lambda pb: _mm(pb, vn)])
            o_ref[rows, lanes] = jnp.where(first_half, o[:t], o[t:]).astype(BF16)


def _attn_sample(q, k_new, v_new, cache_kt, cache_vt, li, bias_c, bias_n, nb, t, *, ns=4):
    lc = cache_kt.shape[3]
    assert nb % ns == 0
    blk = pl.BlockSpec((ns * t, WA), lambda b: (b, 0))
    cblk = pl.BlockSpec((1, ns, WA, lc), lambda b: (li, b, 0, 0))
    return pl.pallas_call(
        functools.partial(_attn_sample_kernel, ns=ns, t=t),
        grid=(nb // ns,),
        in_specs=[blk, blk, blk, cblk, cblk, _full_spec(bias_c), _full_spec(bias_n)],
        out_specs=blk,
        out_shape=jax.ShapeDtypeStruct((nb * t, WA), BF16),
        compiler_params=_params("parallel"),
        name="attn_sample",
    )(q, k_new, v_new, cache_kt, cache_vt, bias_c, bias_n)


def _gla_kernel(q_ref, k_ref, v_ref, la_ref, s0_ref, o_ref, sout_ref, st_scr, *, ns, nblk, ncb, c_len):
    n_pairs = N_HEADS_B // 2
    r_len = ncb * c_len

    @pl.when(pl.program_id(1) == 0)
    def _():
        for s in range(ns):
            for p in range(n_pairs):
                st_scr[s, p] = s0_ref[s, p * LANES:(p + 1) * LANES, :].T

    ri = lax.broadcasted_iota(jnp.int32, (r_len, r_len), 0)
    ci = lax.broadcasted_iota(jnp.int32, (r_len, r_len), 1)
    same_chunk = (ri // c_len) == (ci // c_len)
    causal = same_chunk & (ri >= ci)
    tril = causal.astype(BF16)
    ones_bd = same_chunk.astype(BF16)
    row_chunk = lax.broadcasted_iota(jnp.int32, (r_len, HEAD_V_B), 0) // c_len
    first_half = lax.broadcasted_iota(jnp.int32, (r_len, LANES), 1) < HEAD_K_B
    first_half_sq = lax.broadcasted_iota(jnp.int32, (LANES, LANES), 1) < HEAD_K_B

    def one_block(s, r0):
        rows = pl.ds(r0, r_len)
        la = la_ref[s, rows, :]
        la_hi = la.astype(BF16)
        la_lo = (la - la_hi.astype(F32)).astype(BF16)
        b = _mm(tril, la_hi) + _mm(tril, la_lo)
        bl = _mm(ones_bd, la_hi) + _mm(ones_bd, la_lo)
        ref = 0.5 * bl
        q = q_ref[s, rows, :]
        k = k_ref[s, rows, :]
        qt = (q * jnp.exp(b - ref)).astype(BF16)
        kt = (k * jnp.exp(ref - b)).astype(BF16)
        qd = (q * jnp.exp(b)).astype(BF16)
        kd = (k * jnp.exp(bl - b)).astype(BF16)
        dl = jnp.exp(bl)
        for p in range(n_pairs):
            lanes = slice(p * LANES, (p + 1) * LANES)
            kt2 = kt[:, lanes]
            kd2 = kd[:, lanes]
            zero = jnp.zeros_like(kt2)
            qds, intra, upd = [], [], []
            for e in range(2):
                hm = first_half if e == 0 else ~first_half
                vh = v_ref[s, rows, (2 * p + e) * HEAD_V_B:(2 * p + e + 1) * HEAD_V_B]
                a = jnp.where(causal, _mm_nt(jnp.where(hm, qt[:, lanes], zero), kt2), 0.0).astype(BF16)
                intra.append(_mm(a, vh))
                qds.append(jnp.where(hm, qd[:, lanes], zero))
                v_bd = jnp.concatenate([jnp.where(row_chunk == c, vh, jnp.zeros_like(vh)) for c in range(ncb)], axis=1)
                upd.append(_mm_tn(v_bd, kd2))
            st = st_scr[s, p]
            inter = [[], []]
            for c in range(ncb):
                crow = slice(c * c_len, (c + 1) * c_len)
                q_stack = jnp.concatenate([qds[0][crow], qds[1][crow]], axis=0)
                o_c = _mm_nt(q_stack, st.astype(BF16))
                inter[0].append(o_c[:c_len])
                inter[1].append(o_c[c_len:])
                urow = slice(c * HEAD_V_B, (c + 1) * HEAD_V_B)
                st = st * dl[c * c_len:c * c_len + 1, lanes] + jnp.where(first_half_sq, upd[0][urow], upd[1][urow])
            st_scr[s, p] = st
            for e in range(2):
                vlanes = slice((2 * p + e) * HEAD_V_B, (2 * p + e + 1) * HEAD_V_B)
                o_inter = inter[e][0] if ncb == 1 else jnp.concatenate(inter[e], axis=0)
                o_ref[s, rows, vlanes] = intra[e] + o_inter

    def block(j, carry):
        for s in range(ns):
            one_block(s, pl.multiple_of(j * r_len, r_len))
        return carry

    lax.fori_loop(0, nblk, block, 0)

    @pl.when(pl.program_id(1) == pl.num_programs(1) - 1)
    def _():
        for s in range(ns):
            for p in range(n_pairs):
                sout_ref[s, p * LANES:(p + 1) * LANES, :] = st_scr[s, p].T


def _gla(q, k, v, la, s0, nb, t, c_len, *, ns, tb=512, r_len=256):
    tb = _row_tile(t, tb)
    r_len = _row_tile(tb, r_len)
    assert nb % ns == 0 and r_len % c_len == 0
    kblk = pl.BlockSpec((ns, tb, KB), lambda b, j: (b, j, 0))
    vblk = pl.BlockSpec((ns, tb, VB), lambda b, j: (b, j, 0))
    sblk = pl.BlockSpec((ns, KB, HEAD_V_B), lambda b, j: (b, 0, 0))
    per_stream = lambda a: a.reshape(nb, t, a.shape[-1])
    o, s_out = pl.pallas_call(
        functools.partial(_gla_kernel, ns=ns, nblk=tb // r_len, ncb=r_len // c_len, c_len=c_len),
        grid=(nb // ns, t // tb),
        in_specs=[kblk, kblk, vblk, kblk, sblk],
        out_specs=[vblk, sblk],
        out_shape=[jax.ShapeDtypeStruct((nb, t, VB), F32),
                   jax.ShapeDtypeStruct((nb, KB, HEAD_V_B), F32)],
        scratch_shapes=[pltpu.VMEM((ns, N_HEADS_B // 2, LANES, LANES), F32)],
        compiler_params=_params("parallel", "arbitrary"),
        name="gla",
    )(per_stream(q), per_stream(k), per_stream(v), per_stream(la), s0)
    return o.reshape(nb * t, VB), s_out


def _conv_kernel(x_ref, init_ref, g_ref, w1_ref, b1_ref, wdw_ref, bdw_ref, lng_ref, lnb_ref,
                 w2_ref, b2_ref, o_ref, tail_ref, buf_scr, *, ns, tt):
    d = x_ref.shape[1]
    lo = CONV_HALO - (CONV_W - 1)

    @pl.when(pl.program_id(1) == 0)
    def _():
        for s in range(ns):
            buf_scr[s, lo:CONV_HALO, :] = init_ref[s]
            buf_scr[s, CONV_HALO + tt:CONV_HALO + tt + SUBLANES, :] = jnp.zeros((SUBLANES, d), F32)

    x = x_ref[...]
    u = _mm(_rms(x, g_ref[0]).astype(BF16), w1_ref[0]) + b1_ref[0]
    glu = u[:, :d] * _sigmoid(u[:, d:])
    ys = []
    for s in range(ns):
        buf_scr[s, CONV_HALO:CONV_HALO + tt, :] = glu[s * tt:(s + 1) * tt]
        y = bdw_ref[0]
        for r in range(SUBLANES):
            part = None
            for k in range(lo, lo + CONV_W):
                if k % SUBLANES == r:
                    term = wdw_ref[0, k - lo:k - lo + 1, :] * buf_scr[s, k - r:k - r + tt + SUBLANES, :]
                    part = term if part is None else part + term
            y = y + part[r:r + tt, :]
        ys.append(y)
        tail = buf_scr[s, tt:tt + CONV_HALO, :]
        tail_ref[s] = tail
        buf_scr[s, 0:CONV_HALO, :] = tail
    y = ys[0] if ns == 1 else jnp.concatenate(ys, axis=0)
    mu = jnp.mean(y, axis=-1, keepdims=True)
    yc = y - mu
    var = jnp.mean(yc * yc, axis=-1, keepdims=True)
    yn = yc * lax.rsqrt(var + EPS) * lng_ref[0] + lnb_ref[0]
    z = (yn * _sigmoid(yn)).astype(BF16)
    o_ref[...] = x + _mm(z, w2_ref[0]) + b2_ref[0]


def _conv(x, init, g, l, conv_params, j, nb, t, *, tt=256, ns=8):
    rows, d = x.shape
    tt = _row_tile(t, tt)
    nt = t // tt
    ns = ns if nt == 1 else 1
    assert nb % ns == 0
    xblk = pl.BlockSpec((ns * tt, d), lambda b, i: (b * nt + i, 0))
    return pl.pallas_call(
        functools.partial(_conv_kernel, ns=ns, tt=tt),
        grid=(nb // ns, nt),
        in_specs=[xblk, pl.BlockSpec((ns, CONV_W - 1, d), lambda b, i: (b, 0, 0)), _layer_spec(g, l)]
                 + [_layer_spec(a, j) for a in conv_params],
        out_specs=[xblk, pl.BlockSpec((ns, CONV_HALO, d), lambda b, i: (b, 0, 0))],
        out_shape=[jax.ShapeDtypeStruct((rows, d), F32),
                   jax.ShapeDtypeStruct((nb, CONV_HALO, d), F32)],
        scratch_shapes=[pltpu.VMEM((ns, CONV_HALO + tt + SUBLANES, d), F32)],
        compiler_params=_params("parallel", "arbitrary"),
        name="conv",
    )(x, init, g, *conv_params)


def _band_bias(table):
    n_far = PAD + CHUNK - REL_MAX
    ext = jnp.concatenate([jnp.repeat(table[..., -1:], n_far, axis=-1), table[..., -2::-1]], axis=-1)
    period = ext.shape[-1] + 1
    ext = jnp.pad(ext, ((0, 0), (0, 0), (0, 1)))
    wrapped = jnp.tile(ext, (1, 1, CHUNK + 1))[..., :CHUNK * (period + 1)]
    wrapped = wrapped.reshape(ext.shape[:2] + (CHUNK, period + 1))
    return wrapped[:, :, ::-1, :BAND]


def _group_bias(bias):
    per_chunk = [jnp.pad(bias, ((0, 0), (0, 0), (c * CHUNK, (GROUP - 1 - c) * CHUNK)), constant_values=-1e30)
                 for c in range(GROUP)]
    stacked = jnp.stack(per_chunk, axis=1)
    return stacked.reshape(N_HEADS_A // 2, 2 * GROUP_ROWS, WINDOW)


def _positions_minor(a):
    n_l, n_b, n_t = a.shape[:3]
    return jnp.transpose(a, (0, 1, 3, 4, 2)).reshape(n_l, n_b, WA, n_t)


def _heads_minor(a):
    n_l, n_b, _, n_t = a.shape
    return jnp.transpose(a.reshape(n_l, n_b, N_HEADS_A, HEAD_DIM_A, n_t), (0, 1, 4, 2, 3))


def kernel(x_prompt, x_sample, cache_a_k, cache_a_v, state_b, cache_c, norm_mix_g, norm_ffn_g, w_in_ab, a_q_norm_g, a_k_norm_g, a_rel_bias, b_gate_w_up, b_gate_b, b_out_norm_g, w_out_ab, c_w_pw1, c_b_pw1, c_w_dw, c_b_dw, c_ln_g, c_ln_b, c_w_pw2, c_b_pw2, w_ffn_up, w_ffn_down):
    nbp, tp, d = x_prompt.shape
    nbs, ts, _ = x_sample.shape
    depth = norm_mix_g.shape[0]
    lp = min(PAD, tp)
    lc = cache_a_k.shape[2]
    assert lc == PAD and lp == PAD and ts <= CHUNK and tp % CHUNK == 0
    xp = x_prompt.reshape(nbp * tp, d)
    xs = x_sample.reshape(nbs * ts, d)
    vec = lambda a: a.reshape(a.shape[0], 1, a.shape[1])

    head_id = np.arange(WA) // HEAD_DIM_A
    bd = jnp.asarray(head_id[:, None] == head_id[None, :], BF16)
    g_mix, g_ffn = vec(norm_mix_g), vec(norm_ffn_g)
    w_up, w_down = w_ffn_up.astype(BF16), w_ffn_down.astype(BF16)
    w_in = w_in_ab.astype(BF16)
    w_lr = jnp.pad(w_in_ab[:, :, P_MAIN:], ((0, 0), (0, 0), (0, LANES - GATE_RANK))).astype(BF16)
    gate_w = jnp.pad(b_gate_w_up, ((0, 0), (0, LANES - GATE_RANK), (0, 0))).astype(BF16)
    proj = (w_in, w_lr, gate_w, vec(b_gate_b),
            vec(jnp.tile(a_q_norm_g, (1, N_HEADS_A))), vec(jnp.tile(a_k_norm_g, (1, N_HEADS_A))))
    og = vec(b_out_norm_g)
    w_out = w_out_ab.astype(BF16)
    conv_params = (c_w_pw1.astype(BF16), vec(c_b_pw1), c_w_dw, vec(c_b_dw), vec(c_ln_g), vec(c_ln_b),
                   c_w_pw2.astype(BF16), vec(c_b_pw2))
    bias = _band_bias(a_rel_bias)
    cache_kt, cache_vt = _positions_minor(cache_a_k), _positions_minor(cache_a_v)
    pair_rows = lambda a: a.reshape(N_HEADS_A // 2, 2 * ts, a.shape[-1])

    akp, avp, aks, avs, sbp, sbs, ccp, ccs = [], [], [], [], [], [], [], []
    for l in range(depth):
        if l % 2 == 0:
            i = l // 2
            q, k, v, kt, vt, qb, kb, la, vb, gb = _inproj(xp, g_mix, l, *proj, i, bd, stream_len=tp)
            oa = _attn_prompt(q, k, v, _group_bias(bias[i]), nbp, tp)
            ob, sb = _gla(qb, kb, vb, la, jnp.zeros((nbp, KB, HEAD_V_B), F32), nbp, tp, CHUNK, ns=min(nbp, 2))
            akp.append(kt)
            avp.append(vt)
            sbp.append(sb.reshape(nbp, N_HEADS_B, HEAD_K_B, HEAD_V_B))
            merge_p = (oa, ob, gb, og, w_out, i)

            q, k, v, qb, kb, la, vb, gb = _inproj(xs, g_mix, l, *proj, i, bd)
            oa = _attn_sample(q, k, v, cache_kt, cache_vt, i,
                              pair_rows(bias[i, :, :ts, :lc]), pair_rows(bias[i, :, :ts, lc:lc + ts]), nbs, ts,
                              ns=min(nbs, 4))
            ob, sb = _gla(qb, kb, vb, la, state_b[i].reshape(nbs, KB, HEAD_V_B), nbs, ts, ts, ns=min(nbs, 8))
            aks.append(k.reshape(nbs, ts, N_HEADS_A, HEAD_DIM_A))
            avs.append(v.reshape(nbs, ts, N_HEADS_A, HEAD_DIM_A))
            sbs.append(sb.reshape(nbs, N_HEADS_B, HEAD_K_B, HEAD_V_B))
            merge_s = (oa, ob, gb, og, w_out, i)
        else:
            merge_p = merge_s = None
            j = l // 2
            xp, tail = _conv(xp, jnp.zeros((nbp, CONV_W - 1, d), F32), g_mix, l, conv_params, j, nbp, tp)
            ccp.append(tail[:, CONV_HALO - (CONV_W - 1):])
            xs, tail = _conv(xs, cache_c[j], g_mix, l, conv_params, j, nbs, ts, ns=min(nbs, 8))
            ccs.append(tail[:, CONV_HALO - (CONV_W - 1):])
        xp = _ffn(xp, g_ffn, w_up, w_down, l, merge_p)
        xs = _ffn(xs, g_ffn, w_up, w_down, l, merge_s)
    return (xp.reshape(nbp, tp, d), xs.reshape(nbs, ts, d),
            _heads_minor(jnp.stack(akp)), _heads_minor(jnp.stack(avp)), jnp.stack(aks), jnp.stack(avs),
            jnp.stack(sbp), jnp.stack(sbs), jnp.stack(ccp), jnp.stack(ccs))
```

```python
import functools

import numpy as np
import jax
import jax.numpy as jnp
from jax import lax
from jax.experimental import pallas as pl
from jax.experimental.pallas import tpu as pltpu

F32 = jnp.float32
BF16 = jnp.bfloat16

EPS = 1e-6
CHUNK = 64
PAST_CHUNKS = 8
PAD = PAST_CHUNKS * CHUNK
BAND = PAD + CHUNK
N_HEADS_A = 8
HEAD_DIM_A = 64
REL_MIN = -(CHUNK - 1)
REL_MAX = 2 * CHUNK
N_HEADS_B = 4
HEAD_K_B = 64
HEAD_V_B = 128
GATE_RANK = 16
GATE_NORM = 16.0
CONV_W = 31
CONV_HALO = 32
WA = N_HEADS_A * HEAD_DIM_A
KB = N_HEADS_B * HEAD_K_B
VB = N_HEADS_B * HEAD_V_B
P_MAIN = 3 * WA + 2 * KB + 2 * VB
LANES = 128
SUBLANES = 8
VMEM_LIMIT = 56 * 1024 * 1024


def _mm(a, b):
    return jnp.dot(a, b, preferred_element_type=F32)


def _mm_nt(a, b):
    return lax.dot_general(a, b, (((1,), (1,)), ((), ())), preferred_element_type=F32)


def _mm_tn(a, b):
    return lax.dot_general(a, b, (((0,), (0,)), ((), ())), preferred_element_type=F32)


def _rms(x, g):
    ms = jnp.mean(x * x, axis=-1, keepdims=True)
    return x * lax.rsqrt(ms + EPS) * g


def _sigmoid(x):
    return 1.0 / (1.0 + jnp.exp(-x))


def _params(*sem):
    return pltpu.CompilerParams(dimension_semantics=sem, vmem_limit_bytes=VMEM_LIMIT)


def _row_tile(rows, want):
    tm = min(rows, want)
    assert rows % tm == 0, (rows, tm)
    return tm


def _layer_spec(a, l):
    return pl.BlockSpec((1,) + a.shape[1:], lambda *_: (l,) + (0,) * (a.ndim - 1))


def _full_spec(a):
    return pl.BlockSpec(a.shape, lambda *_: (0,) * a.ndim)


def _ffn_steps(x1, g_ref, wu_ref, wd_ref, o_ref, xn_scr):
    @pl.when(pl.program_id(1) == 0)
    def _():
        x = x1()
        o_ref[...] = x
        xn_scr[...] = _rms(x, g_ref[0]).astype(BF16)

    h = jnp.maximum(_mm(xn_scr[...], wu_ref[0]), 0.0)
    o_ref[...] += _mm((h * h).astype(BF16), wd_ref[0])


def _ffn_kernel(x_ref, g_ref, wu_ref, wd_ref, o_ref, xn_scr):
    _ffn_steps(lambda: x_ref[...], g_ref, wu_ref, wd_ref, o_ref, xn_scr)


def _merge_ffn_kernel(x_ref, oa_ref, ob_ref, gb_ref, og_ref, wa_ref, wb_ref,
                      g_ref, wu_ref, wd_ref, o_ref, xn_scr):
    def merged():
        parts = []
        for h in range(N_HEADS_B):
            lanes = slice(h * HEAD_V_B, (h + 1) * HEAD_V_B)
            gb = gb_ref[:, lanes]
            parts.append((_rms(ob_ref[:, lanes], og_ref[0]) * (gb * _sigmoid(gb))).astype(BF16))
        obn = jnp.concatenate(parts, axis=1)
        return x_ref[...] + _mm(oa_ref[...], wa_ref[0]) + _mm(obn, wb_ref[0])

    _ffn_steps(merged, g_ref, wu_ref, wd_ref, o_ref, xn_scr)


def _ffn(x, g, w_up, w_down, l, merge=None, *, tm=1024, tf=1024):
    rows, d = x.shape
    dff = w_up.shape[2]
    tm = _row_tile(rows, tm)
    row = lambda w: pl.BlockSpec((tm, w), lambda i, j: (i, 0))
    ffn_specs = [_layer_spec(g, l), pl.BlockSpec((1, d, tf), lambda i, j: (l, 0, j)),
                 pl.BlockSpec((1, tf, d), lambda i, j: (l, j, 0))]
    if merge is None:
        body, args, specs = _ffn_kernel, (x,), [row(d)]
    else:
        oa, ob, gb, og, w_out, li = merge
        body, args = _merge_ffn_kernel, (x, oa, ob, gb, og, w_out, w_out)
        specs = [row(d), row(WA), row(VB), row(VB), _layer_spec(og, li),
                 pl.BlockSpec((1, WA, d), lambda i, j: (li, 0, 0)),
                 pl.BlockSpec((1, VB, d), lambda i, j: (li, WA // VB, 0))]
    return pl.pallas_call(
        body,
        grid=(rows // tm, dff // tf),
        in_specs=specs + ffn_specs,
        out_specs=row(d),
        out_shape=jax.ShapeDtypeStruct((rows, d), F32),
        scratch_shapes=[pltpu.VMEM((tm, d), BF16)],
        compiler_params=_params("parallel", "arbitrary"),
        name="ffn" if merge is None else "merge_ffn",
    )(*args, g, w_up, w_down)


def _inproj_kernel(x_ref, g_ref, w_ref, wlr_ref, gw_ref, gbias_ref, qg_ref, kg_ref, bd_ref, *outs,
                   tail_every):
    if tail_every:
        q_o, k_o, v_o, kt_o, vt_o, qb_o, kb_o, la_o, vb_o, gb_o = outs
    else:
        q_o, k_o, v_o, qb_o, kb_o, la_o, vb_o, gb_o = outs
    xn = _rms(x_ref[...], g_ref[0]).astype(BF16)

    def seg(a, b):
        return _mm(xn, w_ref[0, :, a:b])

    def head_norm(h, gain):
        ss = _mm((h * h).astype(BF16), bd_ref[...]) * (1.0 / HEAD_DIM_A)
        return h * lax.rsqrt(ss + EPS) * gain

    q_o[...] = (head_norm(seg(0, WA), qg_ref[0]) * (HEAD_DIM_A ** -0.5)).astype(BF16)
    k = head_norm(seg(WA, 2 * WA), kg_ref[0])
    v = seg(2 * WA, 3 * WA)
    k_o[...] = k.astype(k_o.dtype)
    v_o[...] = v.astype(v_o.dtype)
    if tail_every:
        @pl.when(pl.program_id(0) % tail_every == tail_every - 1)
        def _():
            kt_o[0] = k.T
            vt_o[0] = v.T
    o = 3 * WA
    hb = seg(o, o + 2 * KB)
    qb_o[...] = hb[:, :KB] * (HEAD_K_B ** -0.5)
    kb_o[...] = hb[:, KB:]
    vb_o[...] = seg(o + 2 * KB, o + 2 * KB + VB).astype(BF16)
    gb_o[...] = seg(o + 2 * KB + VB, P_MAIN)
    lr = _mm(xn, wlr_ref[0])
    logit = _mm(lr.astype(BF16), gw_ref[0]) + gbias_ref[0]
    log_sig = jnp.minimum(logit, 0.0) - jnp.log1p(jnp.exp(-jnp.abs(logit)))
    la_o[...] = log_sig * (1.0 / GATE_NORM)


def _inproj(x, g, l, w_in, w_lr, gate_w, gate_b, qg, kg, i, bd, *, stream_len=None, tm=512):
    rows, d = x.shape
    tm = _row_tile(rows, tm)
    row = lambda w: pl.BlockSpec((tm, w), lambda r: (r, 0))
    kv_dtype = BF16 if stream_len else F32
    outs = [(WA, BF16), (WA, kv_dtype), (WA, kv_dtype), (KB, F32), (KB, F32), (KB, F32), (VB, BF16), (VB, F32)]
    out_specs = [row(w) for w, _ in outs]
    out_shape = [jax.ShapeDtypeStruct((rows, w), dt) for w, dt in outs]
    tail_every = None
    if stream_len:
        assert stream_len % tm == 0 and tm == PAD
        tail_every = stream_len // tm
        tail_spec = pl.BlockSpec((1, WA, tm), lambda r: (r // tail_every, 0, 0))
        tail_shape = jax.ShapeDtypeStruct((rows // stream_len, WA, tm), F32)
        out_specs[3:3] = [tail_spec, tail_spec]
        out_shape[3:3] = [tail_shape, tail_shape]
    return pl.pallas_call(
        functools.partial(_inproj_kernel, tail_every=tail_every),
        grid=(rows // tm,),
        in_specs=[row(d), _layer_spec(g, l), _layer_spec(w_in, i), _layer_spec(w_lr, i),
                  _layer_spec(gate_w, i), _layer_spec(gate_b, i), _layer_spec(qg, i),
                  _layer_spec(kg, i), _full_spec(bd)],
        out_specs=out_specs,
        out_shape=out_shape,
        compiler_params=_params("arbitrary"),
        name="inproj",
    )(x, g, w_in, w_lr, gate_w, gate_b, qg, kg, bd)


def _softmax_pv(scores, pv_fns):
    m = scores[0].max(axis=-1, keepdims=True)
    for s in scores[1:]:
        m = jnp.maximum(m, s.max(axis=-1, keepdims=True))
    l = 0.0
    pv = 0.0
    for s, fn in zip(scores, pv_fns):
        p = jnp.exp(s - m)
        l = l + p.sum(axis=-1, keepdims=True)
        pv = pv + fn(p.astype(BF16))
    return pv / l


def _stack_pair(q2, first_half):
    zero = jnp.zeros_like(q2)
    return jnp.concatenate([jnp.where(first_half, q2, zero), jnp.where(first_half, zero, q2)], axis=0)


GROUP = 2
GROUP_ROWS = GROUP * CHUNK
WINDOW = PAD + GROUP_ROWS


def _attn_prompt_kernel(q_ref, k_ref, v_ref, bias_ref, o_ref, *, ng):
    first_half = lax.broadcasted_iota(jnp.int32, (GROUP_ROWS, LANES), 1) < HEAD_DIM_A

    def group(r0, keys, bias_cols):
        for p in range(N_HEADS_A // 2):
            lanes = slice(p * LANES, (p + 1) * LANES)
            kw = k_ref[keys, lanes]
            vw = v_ref[keys, lanes]
            qs = _stack_pair(q_ref[pl.ds(r0, GROUP_ROWS), lanes], first_half)
            s = _mm_nt(qs, kw) + bias_ref[p, :, bias_cols]
            o = _softmax_pv([s], [lambda pb: _mm(pb, vw)])
            o_ref[pl.ds(r0, GROUP_ROWS), lanes] = jnp.where(
                first_half, o[:GROUP_ROWS], o[GROUP_ROWS:]).astype(BF16)

    n_ramp = min(ng, PAD // GROUP_ROWS)
    for g in range(n_ramp):
        n_keys = (g + 1) * GROUP_ROWS
        group(g * GROUP_ROWS, slice(0, n_keys), slice(WINDOW - n_keys, WINDOW))

    def full_group(g, carry):
        r0 = pl.multiple_of(g * GROUP_ROWS, GROUP_ROWS)
        group(r0, pl.ds(pl.multiple_of(r0 - PAD, GROUP_ROWS), WINDOW), slice(0, WINDOW))
        return carry

    lax.fori_loop(n_ramp, ng, full_group, 0, unroll=2)


def _attn_prompt(q, k, v, bias, nb, t):
    ng = t // GROUP_ROWS
    blk = pl.BlockSpec((t, WA), lambda b: (b, 0))
    return pl.pallas_call(
        functools.partial(_attn_prompt_kernel, ng=ng),
        grid=(nb,),
        in_specs=[blk, blk, blk, _full_spec(bias)],
        out_specs=blk,
        out_shape=jax.ShapeDtypeStruct((nb * t, WA), BF16),
        compiler_params=_params("parallel"),
        name="attn_prompt",
    )(q, k, v, bias)


def _attn_sample_kernel(q_ref, kn_ref, vn_ref, ckt_ref, cvt_ref, bias_c_ref, bias_n_ref, o_ref, *, ns, t):
    first_half = lax.broadcasted_iota(jnp.int32, (t, LANES), 1) < HEAD_DIM_A
    for s in range(ns):
        rows = slice(s * t, (s + 1) * t)
        for p in range(N_HEADS_A // 2):
            lanes = slice(p * LANES, (p + 1) * LANES)
            qs = _stack_pair(q_ref[rows, lanes], first_half)
            kct = ckt_ref[0, s, lanes, :].astype(BF16)
            vct = cvt_ref[0, s, lanes, :].astype(BF16)
            kn = kn_ref[rows, lanes].astype(BF16)
            vn = vn_ref[rows, lanes].astype(BF16)
            s_c = _mm(qs, kct) + bias_c_ref[p]
            s_n = _mm_nt(qs, kn) + bias_n_ref[p]
            o = _softmax_pv([s_c, s_n], [lambda pb: _mm_nt(pb, vct), lambda pb: _mm(pb, vn)])
            o_ref[rows, lanes] = jnp.where(first_half, o[:t], o[t:]).astype(BF16)


def _attn_sample(q, k_new, v_new, cache_kt, cache_vt, li, bias_c, bias_n, nb, t, *, ns=4):
    lc = cache_kt.shape[3]
    assert nb % ns == 0
    blk = pl.BlockSpec((ns * t, WA), lambda b: (b, 0))
    cblk = pl.BlockSpec((1, ns, WA, lc), lambda b: (li, b, 0, 0))
    return pl.pallas_call(
        functools.partial(_attn_sample_kernel, ns=ns, t=t),
        grid=(nb // ns,),
        in_specs=[blk, blk, blk, cblk, cblk, _full_spec(bias_c), _full_spec(bias_n)],
        out_specs=blk,
        out_shape=jax.ShapeDtypeStruct((nb * t, WA), BF16),
        compiler_params=_params("parallel"),
        name="attn_sample",
    )(q, k_new, v_new, cache_kt, cache_vt, bias_c, bias_n)


def _gla_kernel(q_ref, k_ref, v_ref, la_ref, s0_ref, o_ref, sout_ref, st_scr, *, ns, nblk, ncb, c_len):
    n_pairs = N_HEADS_B // 2
    r_len = ncb * c_len

    @pl.when(pl.program_id(1) == 0)
    def _():
        for s in range(ns):
            for p in range(n_pairs):
                st_scr[s, p] = s0_ref[s, p * LANES:(p + 1) * LANES, :].T

    ri = lax.broadcasted_iota(jnp.int32, (r_len, r_len), 0)
    ci = lax.broadcasted_iota(jnp.int32, (r_len, r_len), 1)
    same_chunk = (ri // c_len) == (ci // c_len)
    causal = same_chunk & (ri >= ci)
    tril = causal.astype(BF16)
    ones_bd = same_chunk.astype(BF16)
    row_chunk = lax.broadcasted_iota(jnp.int32, (r_len, HEAD_V_B), 0) // c_len
    first_half = lax.broadcasted_iota(jnp.int32, (r_len, LANES), 1) < HEAD_K_B
    first_half_sq = lax.broadcasted_iota(jnp.int32, (LANES, LANES), 1) < HEAD_K_B

    def one_block(s, r0):
        rows = pl.ds(r0, r_len)
        la = la_ref[s, rows, :]
        la_hi = la.astype(BF16)
        la_lo = (la - la_hi.astype(F32)).astype(BF16)
        b = _mm(tril, la_hi) + _mm(tril, la_lo)
        bl = _mm(ones_bd, la_hi) + _mm(ones_bd, la_lo)
        ref = 0.5 * bl
        q = q_ref[s, rows, :]
        k = k_ref[s, rows, :]
        qt = (q * jnp.exp(b - ref)).astype(BF16)
        kt = (k * jnp.exp(ref - b)).astype(BF16)
        qd = (q * jnp.exp(b)).astype(BF16)
        kd = (k * jnp.exp(bl - b)).astype(BF16)
        dl = jnp.exp(bl)
        for p in range(n_pairs):
            lanes = slice(p * LANES, (p + 1) * LANES)
            kt2 = kt[:, lanes]
            kd2 = kd[:, lanes]
            zero = jnp.zeros_like(kt2)
            qds, intra, upd = [], [], []
            for e in range(2):
                hm = first_half if e == 0 else ~first_half
                vh = v_ref[s, rows, (2 * p + e) * HEAD_V_B:(2 * p + e + 1) * HEAD_V_B]
                a = jnp.where(causal, _mm_nt(jnp.where(hm, qt[:, lanes], zero), kt2), 0.0).astype(BF16)
                intra.append(_mm(a, vh))
                qds.append(jnp.where(hm, qd[:, lanes], zero))
                v_bd = jnp.concatenate([jnp.where(row_chunk == c, vh, jnp.zeros_like(vh)) for c in range(ncb)], axis=1)
                upd.append(_mm_tn(v_bd, kd2))
            st = st_scr[s, p]
            inter = [[], []]
            for c in range(ncb):
                crow = slice(c * c_len, (c + 1) * c_len)
                q_stack = jnp.concatenate([qds[0][crow], qds[1][crow]], axis=0)
                o_c = _mm_nt(q_stack, st.astype(BF16))
                inter[0].append(o_c[:c_len])
                inter[1].append(o_c[c_len:])
                urow = slice(c * HEAD_V_B, (c + 1) * HEAD_V_B)
                st = st * dl[c * c_len:c * c_len + 1, lanes] + jnp.where(first_half_sq, upd[0][urow], upd[1][urow])
            st_scr[s, p] = st
            for e in range(2):
                vlanes = slice((2 * p + e) * HEAD_V_B, (2 * p + e + 1) * HEAD_V_B)
                o_inter = inter[e][0] if ncb == 1 else jnp.concatenate(inter[e], axis=0)
                o_ref[s, rows, vlanes] = intra[e] + o_inter

    def block(j, carry):
        for s in range(ns):
            one_block(s, pl.multiple_of(j * r_len, r_len))
        return carry

    lax.fori_loop(0, nblk, block, 0)

    @pl.when(pl.program_id(1) == pl.num_programs(1) - 1)
    def _():
        for s in range(ns):
            for p in range(n_pairs):
                sout_ref[s, p * LANES:(p + 1) * LANES, :] = st_scr[s, p].T


def _gla(q, k, v, la, s0, nb, t, c_len, *, ns, tb=512, r_len=256):
    tb = _row_tile(t, tb)
    r_len = _row_tile(tb, r_len)
    assert nb % ns == 0 and r_len % c_len == 0
    kblk = pl.BlockSpec((ns, tb, KB), lambda b, j: (b, j, 0))
    vblk = pl.BlockSpec((ns, tb, VB), lambda b, j: (b, j, 0))
    sblk = pl.BlockSpec((ns, KB, HEAD_V_B), lambda b, j: (b, 0, 0))
    per_stream = lambda a: a.reshape(nb, t, a.shape[-1])
    o, s_out = pl.pallas_call(
        functools.partial(_gla_kernel, ns=ns, nblk=tb // r_len, ncb=r_len // c_len, c_len=c_len),
        grid=(nb // ns, t // tb),
        in_specs=[kblk, kblk, vblk, kblk, sblk],
        out_specs=[vblk, sblk],
        out_shape=[jax.ShapeDtypeStruct((nb, t, VB), F32),
                   jax.ShapeDtypeStruct((nb, KB, HEAD_V_B), F32)],
        scratch_shapes=[pltpu.VMEM((ns, N_HEADS_B // 2, LANES, LANES), F32)],
        compiler_params=_params("parallel", "arbitrary"),
        name="gla",
    )(per_stream(q), per_stream(k), per_stream(v), per_stream(la), s0)
    return o.reshape(nb * t, VB), s_out


def _conv_kernel(x_ref, init_ref, g_ref, w1_ref, b1_ref, wdw_ref, bdw_ref, lng_ref, lnb_ref,
                 w2_ref, b2_ref, o_ref, tail_ref, buf_scr, *, ns, tt):
    d = x_ref.shape[1]
    lo = CONV_HALO - (CONV_W - 1)

    @pl.when(pl.program_id(1) == 0)
    def _():
        for s in range(ns):
            buf_scr[s, lo:CONV_HALO, :] = init_ref[s]
            buf_scr[s, CONV_HALO + tt:CONV_HALO + tt + SUBLANES, :] = jnp.zeros((SUBLANES, d), F32)

    x = x_ref[...]
    u = _mm(_rms(x, g_ref[0]).astype(BF16), w1_ref[0]) + b1_ref[0]
    glu = u[:, :d] * _sigmoid(u[:, d:])
    ys = []
    for s in range(ns):
        buf_scr[s, CONV_HALO:CONV_HALO + tt, :] = glu[s * tt:(s + 1) * tt]
        y = bdw_ref[0]
        for r in range(SUBLANES):
            part = None
            for k in range(lo, lo + CONV_W):
                if k % SUBLANES == r:
                    term = wdw_ref[0, k - lo:k - lo + 1, :] * buf_scr[s, k - r:k - r + tt + SUBLANES, :]
                    part = term if part is None else part + term
            y = y + part[r:r + tt, :]
        ys.append(y)
        tail = buf_scr[s, tt:tt + CONV_HALO, :]
        tail_ref[s] = tail
        buf_scr[s, 0:CONV_HALO, :] = tail
    y = ys[0] if ns == 1 else jnp.concatenate(ys, axis=0)
    mu = jnp.mean(y, axis=-1, keepdims=True)
    yc = y - mu
    var = jnp.mean(yc * yc, axis=-1, keepdims=True)
    yn = yc * lax.rsqrt(var + EPS) * lng_ref[0] + lnb_ref[0]
    z = (yn * _sigmoid(yn)).astype(BF16)
    o_ref[...] = x + _mm(z, w2_ref[0]) + b2_ref[0]


def _conv(x, init, g, l, conv_params, j, nb, t, *, tt=512, ns=8):
    rows, d = x.shape
    tt = _row_tile(t, tt)
    nt = t // tt
    ns = ns if nt == 1 else 1
    assert nb % ns == 0
    xblk = pl.BlockSpec((ns * tt, d), lambda b, i: (b * nt + i, 0))
    return pl.pallas_call(
        functools.partial(_conv_kernel, ns=ns, tt=tt),
        grid=(nb // ns, nt),
        in_specs=[xblk, pl.BlockSpec((ns, CONV_W - 1, d), lambda b, i: (b, 0, 0)), _layer_spec(g, l)]
                 + [_layer_spec(a, j) for a in conv_params],
        out_specs=[xblk, pl.BlockSpec((ns, CONV_HALO, d), lambda b, i: (b, 0, 0))],
        out_shape=[jax.ShapeDtypeStruct((rows, d), F32),
                   jax.ShapeDtypeStruct((nb, CONV_HALO, d), F32)],
        scratch_shapes=[pltpu.VMEM((ns, CONV_HALO + tt + SUBLANES, d), F32)],
        compiler_params=_params("parallel", "arbitrary"),
        name="conv",
    )(x, init, g, *conv_params)


def _band_bias(table):
    n_far = PAD + CHUNK - REL_MAX
    ext = jnp.concatenate([jnp.repeat(table[..., -1:], n_far, axis=-1), table[..., -2::-1]], axis=-1)
    period = ext.shape[-1] + 1
    ext = jnp.pad(ext, ((0, 0), (0, 0), (0, 1)))
    wrapped = jnp.tile(ext, (1, 1, CHUNK + 1))[..., :CHUNK * (period + 1)]
    wrapped = wrapped.reshape(ext.shape[:2] + (CHUNK, period + 1))
    return wrapped[:, :, ::-1, :BAND]


def _group_bias(bias):
    per_chunk = [jnp.pad(bias, ((0, 0), (0, 0), (c * CHUNK, (GROUP - 1 - c) * CHUNK)), constant_values=-1e30)
                 for c in range(GROUP)]
    stacked = jnp.stack(per_chunk, axis=1)
    return stacked.reshape(N_HEADS_A // 2, 2 * GROUP_ROWS, WINDOW)


def _positions_minor(a):
    n_l, n_b, n_t = a.shape[:3]
    return jnp.transpose(a, (0, 1, 3, 4, 2)).reshape(n_l, n_b, WA, n_t)


def _heads_minor(a):
    n_l, n_b, _, n_t = a.shape
    return jnp.transpose(a.reshape(n_l, n_b, N_HEADS_A, HEAD_DIM_A, n_t), (0, 1, 4, 2, 3))


def kernel(x_prompt, x_sample, cache_a_k, cache_a_v, state_b, cache_c, norm_mix_g, norm_ffn_g, w_in_ab, a_q_norm_g, a_k_norm_g, a_rel_bias, b_gate_w_up, b_gate_b, b_out_norm_g, w_out_ab, c_w_pw1, c_b_pw1, c_w_dw, c_b_dw, c_ln_g, c_ln_b, c_w_pw2, c_b_pw2, w_ffn_up, w_ffn_down):
    nbp, tp, d = x_prompt.shape
    nbs, ts, _ = x_sample.shape
    depth = norm_mix_g.shape[0]
    lp = min(PAD, tp)
    lc = cache_a_k.shape[2]
    assert lc == PAD and lp == PAD and ts <= CHUNK and tp % CHUNK == 0
    xp = x_prompt.reshape(nbp * tp, d)
    xs = x_sample.reshape(nbs * ts, d)
    vec = lambda a: a.reshape(a.shape[0], 1, a.shape[1])

    head_id = np.arange(WA) // HEAD_DIM_A
    bd = jnp.asarray(head_id[:, None] == head_id[None, :], BF16)
    g_mix, g_ffn = vec(norm_mix_g), vec(norm_ffn_g)
    w_up, w_down = w_ffn_up.astype(BF16), w_ffn_down.astype(BF16)
    w_in = w_in_ab.astype(BF16)
    w_lr = jnp.pad(w_in_ab[:, :, P_MAIN:], ((0, 0), (0, 0), (0, LANES - GATE_RANK))).astype(BF16)
    gate_w = jnp.pad(b_gate_w_up, ((0, 0), (0, LANES - GATE_RANK), (0, 0))).astype(BF16)
    proj = (w_in, w_lr, gate_w, vec(b_gate_b),
            vec(jnp.tile(a_q_norm_g, (1, N_HEADS_A))), vec(jnp.tile(a_k_norm_g, (1, N_HEADS_A))))
    og = vec(b_out_norm_g)
    w_out = w_out_ab.astype(BF16)
    conv_params = (c_w_pw1.astype(BF16), vec(c_b_pw1), c_w_dw, vec(c_b_dw), vec(c_ln_g), vec(c_ln_b),
                   c_w_pw2.astype(BF16), vec(c_b_pw2))
    bias = _band_bias(a_rel_bias)
    cache_kt, cache_vt = _positions_minor(cache_a_k), _positions_minor(cache_a_v)
    pair_rows = lambda a: a.reshape(N_HEADS_A // 2, 2 * ts, a.shape[-1])

    akp, avp, aks, avs, sbp, sbs, ccp, ccs = [], [], [], [], [], [], [], []
    for l in range(depth):
        if l % 2 == 0:
            i = l // 2
            q, k, v, kt, vt, qb, kb, la, vb, gb = _inproj(xp, g_mix, l, *proj, i, bd, stream_len=tp)
            oa = _attn_prompt(q, k, v, _group_bias(bias[i]), nbp, tp)
            ob, sb = _gla(qb, kb, vb, la, jnp.zeros((nbp, KB, HEAD_V_B), F32), nbp, tp, CHUNK, ns=min(nbp, 2))
            akp.append(kt)
            avp.append(vt)
            sbp.append(sb.reshape(nbp, N_HEADS_B, HEAD_K_B, HEAD_V_B))
            merge_p = (oa, ob, gb, og, w_out, i)

            q, k, v, qb, kb, la, vb, gb = _inproj(xs, g_mix, l, *proj, i, bd)
            oa = _attn_sample(q, k, v, cache_kt, cache_vt, i,
                              pair_rows(bias[i, :, :ts, :lc]), pair_rows(bias[i, :, :ts, lc:lc + ts]), nbs, ts,
                              ns=min(nbs, 4))
            ob, sb = _gla(qb, kb, vb, la, state_b[i].reshape(nbs, KB, HEAD_V_B), nbs, ts, ts, ns=min(nbs, 8))
            aks.append(k.reshape(nbs, ts, N_HEADS_A, HEAD_DIM_A))
            avs.append(v.reshape(nbs, ts, N_HEADS_A, HEAD_DIM_A))
            sbs.append(sb.reshape(nbs, N_HEADS_B, HEAD_K_B, HEAD_V_B))
            merge_s = (oa, ob, gb, og, w_out, i)
        else:
            merge_p = merge_s = None
            j = l // 2
            xp, tail = _conv(xp, jnp.zeros((nbp, CONV_W - 1, d), F32), g_mix, l, conv_params, j, nbp, tp)
            ccp.append(tail[:, CONV_HALO - (CONV_W - 1):])
            xs, tail = _conv(xs, cache_c[j], g_mix, l, conv_params, j, nbs, ts, ns=min(nbs, 8))
            ccs.append(tail[:, CONV_HALO - (CONV_W - 1):])
        xp = _ffn(xp, g_ffn, w_up, w_down, l, merge_p)
        xs = _ffn(xs, g_ffn, w_up, w_down, l, merge_s)
    return (xp.reshape(nbp, tp, d), xs.reshape(nbs, ts, d),
            _heads_minor(jnp.stack(akp)), _heads_minor(jnp.stack(avp)), jnp.stack(aks), jnp.stack(avs),
            jnp.stack(sbp), jnp.stack(sbs), jnp.stack(ccp), jnp.stack(ccs))
```

```python
import functools

import numpy as np
import jax
import jax.numpy as jnp
from jax import lax
from jax.experimental import pallas as pl
from jax.experimental.pallas import tpu as pltpu

F32 = jnp.float32
BF16 = jnp.bfloat16

EPS = 1e-6
CHUNK = 64
PAST_CHUNKS = 8
PAD = PAST_CHUNKS * CHUNK
BAND = PAD + CHUNK
N_HEADS_A = 8
HEAD_DIM_A = 64
REL_MIN = -(CHUNK - 1)
REL_MAX = 2 * CHUNK
N_HEADS_B = 4
HEAD_K_B = 64
HEAD_V_B = 128
GATE_RANK = 16
GATE_NORM = 16.0
CONV_W = 31
CONV_HALO = 32
WA = N_HEADS_A * HEAD_DIM_A
KB = N_HEADS_B * HEAD_K_B
VB = N_HEADS_B * HEAD_V_B
P_MAIN = 3 * WA + 2 * KB + 2 * VB
LANES = 128
SUBLANES = 8
MXU_TILE = 256
VMEM_LIMIT = 56 * 1024 * 1024


def _mm(a, b):
    return jnp.dot(a, b, preferred_element_type=F32)


def _mm_nt(a, b):
    return lax.dot_general(a, b, (((1,), (1,)), ((), ())), preferred_element_type=F32)


def _mm_tn(a, b):
    return lax.dot_general(a, b, (((0,), (0,)), ((), ())), preferred_element_type=F32)


def _rms(x, g):
    ms = jnp.mean(x * x, axis=-1, keepdims=True)
    return x * lax.rsqrt(ms + EPS) * g


def _sigmoid(x):
    return 1.0 / (1.0 + jnp.exp(-x))


def _params(*sem):
    return pltpu.CompilerParams(dimension_semantics=sem, vmem_limit_bytes=VMEM_LIMIT)


def _row_tile(rows, want):
    tm = min(rows, want)
    assert rows % tm == 0, (rows, tm)
    return tm


def _layer_spec(a, l):
    return pl.BlockSpec((1,) + a.shape[1:], lambda *_: (l,) + (0,) * (a.ndim - 1))


def _full_spec(a):
    return pl.BlockSpec(a.shape, lambda *_: (0,) * a.ndim)


def _ffn_steps(x1, g_ref, wu_ref, wd_ref, o_ref, xn_scr):
    @pl.when(pl.program_id(1) == 0)
    def _():
        x = x1()
        o_ref[...] = x
        xn_scr[...] = _rms(x, g_ref[0]).astype(BF16)

    h = jnp.maximum(_mm(xn_scr[...], wu_ref[0]), 0.0)
    o_ref[...] += _mm((h * h).astype(BF16), wd_ref[0])


def _ffn_kernel(x_ref, g_ref, wu_ref, wd_ref, o_ref, xn_scr):
    _ffn_steps(lambda: x_ref[...], g_ref, wu_ref, wd_ref, o_ref, xn_scr)


def _merge_ffn_kernel(x_ref, oa_ref, ob_ref, wa_ref, wb_ref, g_ref, wu_ref, wd_ref, o_ref, xn_scr):
    def merged():
        return x_ref[...] + _mm(oa_ref[...], wa_ref[0]) + _mm(ob_ref[...], wb_ref[0])

    _ffn_steps(merged, g_ref, wu_ref, wd_ref, o_ref, xn_scr)


def _ffn(x, g, w_up, w_down, l, merge=None, *, tm=1024, tf=1024):
    rows, d = x.shape
    dff = w_up.shape[2]
    tm = _row_tile(rows, tm)
    row = lambda w: pl.BlockSpec((tm, w), lambda i, j: (i, 0))
    ffn_specs = [_layer_spec(g, l), pl.BlockSpec((1, d, tf), lambda i, j: (l, 0, j)),
                 pl.BlockSpec((1, tf, d), lambda i, j: (l, j, 0))]
    if merge is None:
        body, args, specs = _ffn_kernel, (x,), [row(d)]
    else:
        oa, ob, w_out, li = merge
        body, args = _merge_ffn_kernel, (x, oa, ob, w_out, w_out)
        specs = [row(d), row(WA), row(VB),
                 pl.BlockSpec((1, WA, d), lambda i, j: (li, 0, 0)),
                 pl.BlockSpec((1, VB, d), lambda i, j: (li, WA // VB, 0))]
    return pl.pallas_call(
        body,
        grid=(rows // tm, dff // tf),
        in_specs=specs + ffn_specs,
        out_specs=row(d),
        out_shape=jax.ShapeDtypeStruct((rows, d), F32),
        scratch_shapes=[pltpu.VMEM((tm, d), BF16)],
        compiler_params=_params("parallel", "arbitrary"),
        name="ffn" if merge is None else "merge_ffn",
    )(*args, g, w_up, w_down)


def _inproj_kernel(x_ref, g_ref, w_ref, wlr_ref, gw_ref, gbias_ref, qg_ref, kg_ref, bd_ref, *outs,
                   tail_every):
    if tail_every:
        q_o, k_o, v_o, kt_o, vt_o, qb_o, kb_o, la_o, vb_o, gb_o = outs
    else:
        q_o, k_o, v_o, qb_o, kb_o, la_o, vb_o, gb_o = outs
    xn = _rms(x_ref[...], g_ref[0]).astype(BF16)

    def seg(a, b):
        return _mm(xn, w_ref[0, :, a:b])

    def head_norm(h, gain):
        sq = (h * h).astype(BF16)
        w = bd_ref.shape[0]
        ss = jnp.concatenate([_mm(sq[:, c:c + w], bd_ref[...]) for c in range(0, WA, w)], axis=1)
        return h * lax.rsqrt(ss * (1.0 / HEAD_DIM_A) + EPS) * gain

    q_o[...] = (head_norm(seg(0, WA), qg_ref[0]) * (HEAD_DIM_A ** -0.5)).astype(BF16)
    k = head_norm(seg(WA, 2 * WA), kg_ref[0])
    v = seg(2 * WA, 3 * WA)
    k_o[...] = k.astype(k_o.dtype)
    v_o[...] = v.astype(v_o.dtype)
    if tail_every:
        @pl.when(pl.program_id(0) % tail_every == tail_every - 1)
        def _():
            kt_o[0] = k.T
            vt_o[0] = v.T
    o = 3 * WA
    hb = seg(o, o + 2 * KB)
    qb_o[...] = hb[:, :KB] * (HEAD_K_B ** -0.5)
    kb_o[...] = hb[:, KB:]
    vb_o[...] = seg(o + 2 * KB, o + 2 * KB + VB).astype(BF16)
    gb_o[...] = seg(o + 2 * KB + VB, P_MAIN)
    lr = _mm(xn, wlr_ref[0])
    logit = _mm(lr.astype(BF16), gw_ref[0]) + gbias_ref[0]
    log_sig = jnp.minimum(logit, 0.0) - jnp.log1p(jnp.exp(-jnp.abs(logit)))
    la_o[...] = log_sig * (1.0 / GATE_NORM)


def _inproj(x, g, l, w_in, w_lr, gate_w, gate_b, qg, kg, i, bd, *, stream_len=None, tm=512):
    rows, d = x.shape
    tm = _row_tile(rows, tm)
    row = lambda w: pl.BlockSpec((tm, w), lambda r: (r, 0))
    kv_dtype = BF16 if stream_len else F32
    outs = [(WA, BF16), (WA, kv_dtype), (WA, kv_dtype), (KB, F32), (KB, F32), (KB, F32), (VB, BF16), (VB, F32)]
    out_specs = [row(w) for w, _ in outs]
    out_shape = [jax.ShapeDtypeStruct((rows, w), dt) for w, dt in outs]
    tail_every = None
    if stream_len:
        assert stream_len % tm == 0 and tm == PAD
        tail_every = stream_len // tm
        tail_spec = pl.BlockSpec((1, WA, tm), lambda r: (r // tail_every, 0, 0))
        tail_shape = jax.ShapeDtypeStruct((rows // stream_len, WA, tm), F32)
        out_specs[3:3] = [tail_spec, tail_spec]
        out_shape[3:3] = [tail_shape, tail_shape]
    return pl.pallas_call(
        functools.partial(_inproj_kernel, tail_every=tail_every),
        grid=(rows // tm,),
        in_specs=[row(d), _layer_spec(g, l), _layer_spec(w_in, i), _layer_spec(w_lr, i),
                  _layer_spec(gate_w, i), _layer_spec(gate_b, i), _layer_spec(qg, i),
                  _layer_spec(kg, i), _full_spec(bd)],
        out_specs=out_specs,
        out_shape=out_shape,
        compiler_params=_params("arbitrary"),
        name="inproj",
    )(x, g, w_in, w_lr, gate_w, gate_b, qg, kg, bd)


def _softmax_pv(scores, pv_fns):
    m = scores[0].max(axis=-1, keepdims=True)
    for s in scores[1:]:
        m = jnp.maximum(m, s.max(axis=-1, keepdims=True))
    l = 0.0
    pv = 0.0
    for s, fn in zip(scores, pv_fns):
        p = jnp.exp(s - m)
        l = l + p.sum(axis=-1, keepdims=True)
        pv = pv + fn(p.astype(BF16))
    return pv / l


def _stack_pair(q2, first_half):
    zero = jnp.zeros_like(q2)
    return jnp.concatenate([jnp.where(first_half, q2, zero), jnp.where(first_half, zero, q2)], axis=0)


GROUP = 4
GROUP_ROWS = GROUP * CHUNK
WINDOW = PAD + GROUP_ROWS


def _attn_prompt_kernel(q_ref, k_ref, v_ref, bias_ref, o_ref, *, ng):
    first_half = lax.broadcasted_iota(jnp.int32, (GROUP_ROWS, LANES), 1) < HEAD_DIM_A

    def group(r0, keys, bias_cols):
        for p in range(N_HEADS_A // 2):
            lanes = slice(p * LANES, (p + 1) * LANES)
            kw = k_ref[keys, lanes]
            vw = v_ref[keys, lanes]
            qs = _stack_pair(q_ref[pl.ds(r0, GROUP_ROWS), lanes], first_half)
            s = _mm_nt(qs, kw) + bias_ref[p, :, bias_cols]
            o = _softmax_pv([s], [lambda pb: _mm(pb, vw)])
            o_ref[pl.ds(r0, GROUP_ROWS), lanes] = jnp.where(
                first_half, o[:GROUP_ROWS], o[GROUP_ROWS:]).astype(BF16)

    n_ramp = min(ng, PAD // GROUP_ROWS)
    for g in range(n_ramp):
        n_keys = (g + 1) * GROUP_ROWS
        group(g * GROUP_ROWS, slice(0, n_keys), slice(WINDOW - n_keys, WINDOW))

    def full_group(g, carry):
        r0 = pl.multiple_of(g * GROUP_ROWS, GROUP_ROWS)
        group(r0, pl.ds(pl.multiple_of(r0 - PAD, GROUP_ROWS), WINDOW), slice(0, WINDOW))
        return carry

    lax.fori_loop(n_ramp, ng, full_group, 0, unroll=2)


def _attn_prompt(q, k, v, bias, nb, t):
    ng = t // GROUP_ROWS
    blk = pl.BlockSpec((t, WA), lambda b: (b, 0))
    return pl.pallas_call(
        functools.partial(_attn_prompt_kernel, ng=ng),
        grid=(nb,),
        in_specs=[blk, blk, blk, _full_spec(bias)],
        out_specs=blk,
        out_shape=jax.ShapeDtypeStruct((nb * t, WA), BF16),
        compiler_params=_params("parallel"),
        name="attn_prompt",
    )(q, k, v, bias)


def _attn_sample_kernel(q_ref, kn_ref, vn_ref, ckt_ref, cvt_ref, bias_c_ref, bias_n_ref, o_ref, *, ns, t):
    first_half = lax.broadcasted_iota(jnp.int32, (t, LANES), 1) < HEAD_DIM_A
    for s in range(ns):
        rows = slice(s * t, (s + 1) * t)
        for p in range(N_HEADS_A // 2):
            lanes = slice(p * LANES, (p + 1) * LANES)
            qs = _stack_pair(q_ref[rows, lanes], first_half)
            kct = ckt_ref[0, s, lanes, :].astype(BF16)
            vct = cvt_ref[0, s, lanes, :].astype(BF16)
            kn = kn_ref[rows, lanes].astype(BF16)
            vn = vn_ref[rows, lanes].astype(BF16)
            s_c = _mm(qs, kct) + bias_c_ref[p]
            s_n = _mm_nt(qs, kn) + bias_n_ref[p]
            o = _softmax_pv([s_c, s_n], [lambda pb: _mm_nt(pb, vct), lambda pb: _mm(pb, vn)])
            o_ref[rows, lanes] = jnp.where(first_half, o[:t], o[t:]).astype(BF16)


def _attn_sample(q, k_new, v_new, cache_kt, cache_vt, li, bias_c, bias_n, nb, t, *, ns=4):
    lc = cache_kt.shape[3]
    assert nb % ns == 0
    blk = pl.BlockSpec((ns * t, WA), lambda b: (b, 0))
    cblk = pl.BlockSpec((1, ns, WA, lc), lambda b: (li, b, 0, 0))
    return pl.pallas_call(
        functools.partial(_attn_sample_kernel, ns=ns, t=t),
        grid=(nb // ns,),
        in_specs=[blk, blk, blk, cblk, cblk, _full_spec(bias_c), _full_spec(bias_n)],
        out_specs=blk,
        out_shape=jax.ShapeDtypeStruct((nb * t, WA), BF16),
        compiler_params=_params("parallel"),
        name="attn_sample",
    )(q, k_new, v_new, cache_kt, cache_vt, bias_c, bias_n)


def _gla_kernel(q_ref, k_ref, v_ref, la_ref, gb_ref, og_ref, s0_ref, o_ref, sout_ref, st_scr,
                *, ns, nblk, ncb, c_len):
    n_pairs = N_HEADS_B // 2
    r_len = ncb * c_len

    @pl.when(pl.program_id(1) == 0)
    def _():
        for s in range(ns):
            for p in range(n_pairs):
                st_scr[s, p] = s0_ref[s, p * LANES:(p + 1) * LANES, :].T

    ri = lax.broadcasted_iota(jnp.int32, (r_len, r_len), 0)
    ci = lax.broadcasted_iota(jnp.int32, (r_len, r_len), 1)
    same_chunk = (ri // c_len) == (ci // c_len)
    causal = same_chunk & (ri >= ci)
    causal2 = jnp.concatenate([causal, causal], axis=0)
    tril = causal.astype(BF16)
    ones_bd = same_chunk.astype(BF16)
    row_chunk = lax.broadcasted_iota(jnp.int32, (r_len, HEAD_V_B), 0) // c_len
    first_half = lax.broadcasted_iota(jnp.int32, (r_len, LANES), 1) < HEAD_K_B
    first_half_sq = lax.broadcasted_iota(jnp.int32, (LANES, LANES), 1) < HEAD_K_B

    def one_block(s, r0):
        rows = pl.ds(r0, r_len)
        la = la_ref[s, rows, :]
        la_hi = la.astype(BF16)
        la_lo = (la - la_hi.astype(F32)).astype(BF16)
        b = _mm(tril, la_hi) + _mm(tril, la_lo)
        bl = _mm(ones_bd, la_hi) + _mm(ones_bd, la_lo)
        ref = 0.5 * bl
        q = q_ref[s, rows, :]
        k = k_ref[s, rows, :]
        qt = (q * jnp.exp(b - ref)).astype(BF16)
        kt = (k * jnp.exp(ref - b)).astype(BF16)
        qd = (q * jnp.exp(b)).astype(BF16)
        kd = (k * jnp.exp(bl - b)).astype(BF16)
        dl = jnp.exp(bl)
        for p in range(n_pairs):
            lanes = slice(p * LANES, (p + 1) * LANES)
            kt2 = kt[:, lanes]
            kd2 = kd[:, lanes]
            vlanes = [slice((2 * p + e) * HEAD_V_B, (2 * p + e + 1) * HEAD_V_B) for e in range(2)]
            vs = [v_ref[s, rows, vl] for vl in vlanes]
            a = jnp.where(causal2, _mm_nt(_stack_pair(qt[:, lanes], first_half), kt2), 0.0).astype(BF16)
            intra = [_mm(a[e * r_len:(e + 1) * r_len], vs[e]) for e in range(2)]
            qd_stack = _stack_pair(qd[:, lanes], first_half)
            v_bd = jnp.concatenate([jnp.where(row_chunk == c, vs[e], jnp.zeros_like(vs[e]))
                                    for e in range(2) for c in range(ncb)], axis=1)
            upd = _mm_tn(v_bd, kd2)
            st = st_scr[s, p]
            inter = [[], []]
            for c in range(ncb):
                q_c = jnp.concatenate([qd_stack[e * r_len + c * c_len:e * r_len + (c + 1) * c_len] for e in range(2)], axis=0)
                o_c = _mm_nt(q_c, st.astype(BF16))
                inter[0].append(o_c[:c_len])
                inter[1].append(o_c[c_len:])
                u0 = upd[c * HEAD_V_B:(c + 1) * HEAD_V_B]
                u1 = upd[(ncb + c) * HEAD_V_B:(ncb + c + 1) * HEAD_V_B]
                st = st * dl[c * c_len:c * c_len + 1, lanes] + jnp.where(first_half_sq, u0, u1)
            st_scr[s, p] = st
            for e in range(2):
                o_inter = inter[e][0] if ncb == 1 else jnp.concatenate(inter[e], axis=0)
                gate = gb_ref[s, rows, vlanes[e]]
                o = _rms(intra[e] + o_inter, og_ref[0]) * (gate * _sigmoid(gate))
                o_ref[s, rows, vlanes[e]] = o.astype(BF16)

    def block(j, carry):
        for s in range(ns):
            one_block(s, pl.multiple_of(j * r_len, r_len))
        return carry

    lax.fori_loop(0, nblk, block, 0)

    @pl.when(pl.program_id(1) == pl.num_programs(1) - 1)
    def _():
        for s in range(ns):
            for p in range(n_pairs):
                sout_ref[s, p * LANES:(p + 1) * LANES, :] = st_scr[s, p].T


def _gla(q, k, v, la, gb, og, li, s0, nb, t, c_len, *, ns, tb=512, r_len=256):
    tb = _row_tile(t, tb)
    r_len = _row_tile(tb, r_len)
    assert nb % ns == 0 and r_len % c_len == 0
    kblk = pl.BlockSpec((ns, tb, KB), lambda b, j: (b, j, 0))
    vblk = pl.BlockSpec((ns, tb, VB), lambda b, j: (b, j, 0))
    sblk = pl.BlockSpec((ns, KB, HEAD_V_B), lambda b, j: (b, 0, 0))
    per_stream = lambda a: a.reshape(nb, t, a.shape[-1])
    o, s_out = pl.pallas_call(
        functools.partial(_gla_kernel, ns=ns, nblk=tb // r_len, ncb=r_len // c_len, c_len=c_len),
        grid=(nb // ns, t // tb),
        in_specs=[kblk, kblk, vblk, kblk, vblk, _layer_spec(og, li), sblk],
        out_specs=[vblk, sblk],
        out_shape=[jax.ShapeDtypeStruct((nb, t, VB), BF16),
                   jax.ShapeDtypeStruct((nb, KB, HEAD_V_B), F32)],
        scratch_shapes=[pltpu.VMEM((ns, N_HEADS_B // 2, LANES, LANES), F32)],
        compiler_params=_params("parallel", "arbitrary"),
        name="gla",
    )(per_stream(q), per_stream(k), per_stream(v), per_stream(la), per_stream(gb), og, s0)
    return o.reshape(nb * t, VB), s_out


def _conv_kernel(x_ref, init_ref, g_ref, w1_ref, b1_ref, wdw_ref, bdw_ref, lng_ref, lnb_ref,
                 w2_ref, b2_ref, o_ref, tail_ref, buf_scr, *, ns, tt):
    d = x_ref.shape[1]
    lo = CONV_HALO - (CONV_W - 1)

    @pl.when(pl.program_id(1) == 0)
    def _():
        for s in range(ns):
            buf_scr[s, lo:CONV_HALO, :] = init_ref[s]
            buf_scr[s, CONV_HALO + tt:CONV_HALO + tt + SUBLANES, :] = jnp.zeros((SUBLANES, d), F32)

    x = x_ref[...]
    u = _mm(_rms(x, g_ref[0]).astype(BF16), w1_ref[0]) + b1_ref[0]
    glu = u[:, :d] * _sigmoid(u[:, d:])
    ys = []
    for s in range(ns):
        buf_scr[s, CONV_HALO:CONV_HALO + tt, :] = glu[s * tt:(s + 1) * tt]
        y = bdw_ref[0]
        for r in range(SUBLANES):
            part = None
            for k in range(lo, lo + CONV_W):
                if k % SUBLANES == r:
                    term = wdw_ref[0, k - lo:k - lo + 1, :] * buf_scr[s, k - r:k - r + tt + SUBLANES, :]
                    part = term if part is None else part + term
            y = y + part[r:r + tt, :]
        ys.append(y)
        tail = buf_scr[s, tt:tt + CONV_HALO, :]
        tail_ref[s] = tail
        buf_scr[s, 0:CONV_HALO, :] = tail
    y = ys[0] if ns == 1 else jnp.concatenate(ys, axis=0)
    mu = jnp.mean(y, axis=-1, keepdims=True)
    yc = y - mu
    var = jnp.mean(yc * yc, axis=-1, keepdims=True)
    yn = yc * lax.rsqrt(var + EPS) * lng_ref[0] + lnb_ref[0]
    z = (yn * _sigmoid(yn)).astype(BF16)
    o_ref[...] = x + _mm(z, w2_ref[0]) + b2_ref[0]


def _conv(x, init, g, l, conv_params, j, nb, t, *, tt=512, ns=8):
    rows, d = x.shape
    tt = _row_tile(t, tt)
    nt = t // tt
    ns = ns if nt == 1 else 1
    assert nb % ns == 0
    xblk = pl.BlockSpec((ns * tt, d), lambda b, i: (b * nt + i, 0))
    return pl.pallas_call(
        functools.partial(_conv_kernel, ns=ns, tt=tt),
        grid=(nb // ns, nt),
        in_specs=[xblk, pl.BlockSpec((ns, CONV_W - 1, d), lambda b, i: (b, 0, 0)), _layer_spec(g, l)]
                 + [_layer_spec(a, j) for a in conv_params],
        out_specs=[xblk, pl.BlockSpec((ns, CONV_HALO, d), lambda b, i: (b, 0, 0))],
        out_shape=[jax.ShapeDtypeStruct((rows, d), F32),
                   jax.ShapeDtypeStruct((nb, CONV_HALO, d), F32)],
        scratch_shapes=[pltpu.VMEM((ns, CONV_HALO + tt + SUBLANES, d), F32)],
        compiler_params=_params("parallel", "arbitrary"),
        name="conv",
    )(x, init, g, *conv_params)


def _band_bias(table):
    n_far = PAD + CHUNK - REL_MAX
    ext = jnp.concatenate([jnp.repeat(table[..., -1:], n_far, axis=-1), table[..., -2::-1]], axis=-1)
    period = ext.shape[-1] + 1
    ext = jnp.pad(ext, ((0, 0), (0, 0), (0, 1)))
    wrapped = jnp.tile(ext, (1, 1, CHUNK + 1))[..., :CHUNK * (period + 1)]
    wrapped = wrapped.reshape(ext.shape[:2] + (CHUNK, period + 1))
    return wrapped[:, :, ::-1, :BAND]


def _group_bias(bias):
    per_chunk = [jnp.pad(bias, ((0, 0), (0, 0), (c * CHUNK, (GROUP - 1 - c) * CHUNK)), constant_values=-1e30)
                 for c in range(GROUP)]
    stacked = jnp.stack(per_chunk, axis=1)
    return stacked.reshape(N_HEADS_A // 2, 2 * GROUP_ROWS, WINDOW)


def _positions_minor(a):
    n_l, n_b, n_t = a.shape[:3]
    return jnp.transpose(a, (0, 1, 3, 4, 2)).reshape(n_l, n_b, WA, n_t)


def _heads_minor(a):
    n_l, n_b, _, n_t = a.shape
    return jnp.transpose(a.reshape(n_l, n_b, N_HEADS_A, HEAD_DIM_A, n_t), (0, 1, 4, 2, 3))


def kernel(x_prompt, x_sample, cache_a_k, cache_a_v, state_b, cache_c, norm_mix_g, norm_ffn_g, w_in_ab, a_q_norm_g, a_k_norm_g, a_rel_bias, b_gate_w_up, b_gate_b, b_out_norm_g, w_out_ab, c_w_pw1, c_b_pw1, c_w_dw, c_b_dw, c_ln_g, c_ln_b, c_w_pw2, c_b_pw2, w_ffn_up, w_ffn_down):
    nbp, tp, d = x_prompt.shape
    nbs, ts, _ = x_sample.shape
    depth = norm_mix_g.shape[0]
    lp = min(PAD, tp)
    lc = cache_a_k.shape[2]
    assert lc == PAD and lp == PAD and ts <= CHUNK and tp % CHUNK == 0
    xp = x_prompt.reshape(nbp * tp, d)
    xs = x_sample.reshape(nbs * ts, d)
    vec = lambda a: a.reshape(a.shape[0], 1, a.shape[1])

    head_id = np.arange(MXU_TILE) // HEAD_DIM_A
    bd = jnp.asarray(head_id[:, None] == head_id[None, :], BF16)
    g_mix, g_ffn = vec(norm_mix_g), vec(norm_ffn_g)
    w_up, w_down = w_ffn_up.astype(BF16), w_ffn_down.astype(BF16)
    w_in = w_in_ab.astype(BF16)
    w_lr = jnp.pad(w_in_ab[:, :, P_MAIN:], ((0, 0), (0, 0), (0, LANES - GATE_RANK))).astype(BF16)
    gate_w = jnp.pad(b_gate_w_up, ((0, 0), (0, LANES - GATE_RANK), (0, 0))).astype(BF16)
    proj = (w_in, w_lr, gate_w, vec(b_gate_b),
            vec(jnp.tile(a_q_norm_g, (1, N_HEADS_A))), vec(jnp.tile(a_k_norm_g, (1, N_HEADS_A))))
    og = vec(b_out_norm_g)
    w_out = w_out_ab.astype(BF16)
    conv_params = (c_w_pw1.astype(BF16), vec(c_b_pw1), c_w_dw, vec(c_b_dw), vec(c_ln_g), vec(c_ln_b),
                   c_w_pw2.astype(BF16), vec(c_b_pw2))
    bias = _band_bias(a_rel_bias)
    cache_kt, cache_vt = _positions_minor(cache_a_k), _positions_minor(cache_a_v)
    pair_rows = lambda a: a.reshape(N_HEADS_A // 2, 2 * ts, a.shape[-1])

    akp, avp, aks, avs, sbp, sbs, ccp, ccs = [], [], [], [], [], [], [], []
    for l in range(depth):
        if l % 2 == 0:
            i = l // 2
            q, k, v, kt, vt, qb, kb, la, vb, gb = _inproj(xp, g_mix, l, *proj, i, bd, stream_len=tp)
            oa = _attn_prompt(q, k, v, _group_bias(bias[i]), nbp, tp)
            ob, sb = _gla(qb, kb, vb, la, gb, og, i, jnp.zeros((nbp, KB, HEAD_V_B), F32), nbp, tp, CHUNK,
                          ns=min(nbp, 2))
            akp.append(kt)
            avp.append(vt)
            sbp.append(sb.reshape(nbp, N_HEADS_B, HEAD_K_B, HEAD_V_B))
            merge_p = (oa, ob, w_out, i)

            q, k, v, qb, kb, la, vb, gb = _inproj(xs, g_mix, l, *proj, i, bd)
            oa = _attn_sample(q, k, v, cache_kt, cache_vt, i,
                              pair_rows(bias[i, :, :ts, :lc]), pair_rows(bias[i, :, :ts, lc:lc + ts]), nbs, ts,
                              ns=min(nbs, 4))
            ob, sb = _gla(qb, kb, vb, la, gb, og, i, state_b[i].reshape(nbs, KB, HEAD_V_B), nbs, ts, ts,
                          ns=min(nbs, 8))
            aks.append(k.reshape(nbs, ts, N_HEADS_A, HEAD_DIM_A))
            avs.append(v.reshape(nbs, ts, N_HEADS_A, HEAD_DIM_A))
            sbs.append(sb.reshape(nbs, N_HEADS_B, HEAD_K_B, HEAD_V_B))
            merge_s = (oa, ob, w_out, i)
        else:
            merge_p = merge_s = None
            j = l // 2
            xp, tail = _conv(xp, jnp.zeros((nbp, CONV_W - 1, d), F32), g_mix, l, conv_params, j, nbp, tp)
            ccp.append(tail[:, CONV_HALO - (CONV_W - 1):])
            xs, tail = _conv(xs, cache_c[j], g_mix, l, conv_params, j, nbs, ts, ns=min(nbs, 8))
            ccs.append(tail[:, CONV_HALO - (CONV_W - 1):])
        xp = _ffn(xp, g_ffn, w_up, w_down, l, merge_p)
        xs = _ffn(xs, g_ffn, w_up, w_down, l, merge_s)
    return (xp.reshape(nbp, tp, d), xs.reshape(nbs, ts, d),
            _heads_minor(jnp.stack(akp)), _heads_minor(jnp.stack(avp)), jnp.stack(aks), jnp.stack(avs),
            jnp.stack(sbp), jnp.stack(sbs), jnp.stack(ccp), jnp.stack(ccs))
```

```python
import functools

import numpy as np
import jax
import jax.numpy as jnp
from jax import lax
from jax.experimental import pallas as pl
from jax.experimental.pallas import tpu as pltpu

F32 = jnp.float32
BF16 = jnp.bfloat16

EPS = 1e-6
CHUNK = 64
PAST_CHUNKS = 8
PAD = PAST_CHUNKS * CHUNK
BAND = PAD + CHUNK
N_HEADS_A = 8
HEAD_DIM_A = 64
REL_MIN = -(CHUNK - 1)
REL_MAX = 2 * CHUNK
N_HEADS_B = 4
HEAD_K_B = 64
HEAD_V_B = 128
GATE_RANK = 16
GATE_NORM = 16.0
CONV_W = 31
CONV_HALO = 32
WA = N_HEADS_A * HEAD_DIM_A
KB = N_HEADS_B * HEAD_K_B
VB = N_HEADS_B * HEAD_V_B
P_MAIN = 3 * WA + 2 * KB + 2 * VB
LANES = 128
SUBLANES = 8
MXU_TILE = 256
VMEM_LIMIT = 56 * 1024 * 1024


def _mm(a, b):
    return jnp.dot(a, b, preferred_element_type=F32)


def _mm_nt(a, b):
    return lax.dot_general(a, b, (((1,), (1,)), ((), ())), preferred_element_type=F32)


def _mm_tn(a, b):
    return lax.dot_general(a, b, (((0,), (0,)), ((), ())), preferred_element_type=F32)


def _rms(x, g):
    ms = jnp.mean(x * x, axis=-1, keepdims=True)
    return x * lax.rsqrt(ms + EPS) * g


def _sigmoid(x):
    return 1.0 / (1.0 + jnp.exp(-x))


def _params(*sem):
    return pltpu.CompilerParams(dimension_semantics=sem, vmem_limit_bytes=VMEM_LIMIT)


def _row_tile(rows, want):
    tm = min(rows, want)
    assert rows % tm == 0, (rows, tm)
    return tm


def _layer_spec(a, l):
    return pl.BlockSpec((1,) + a.shape[1:], lambda *_: (l,) + (0,) * (a.ndim - 1))


def _full_spec(a):
    return pl.BlockSpec(a.shape, lambda *_: (0,) * a.ndim)


def _ffn_steps(x1, g_ref, wu_ref, wd_ref, o_ref, xn_scr):
    @pl.when(pl.program_id(1) == 0)
    def _():
        x = x1()
        o_ref[...] = x
        xn_scr[...] = _rms(x, g_ref[0]).astype(BF16)

    h = jnp.maximum(_mm(xn_scr[...], wu_ref[0]), 0.0)
    o_ref[...] += _mm((h * h).astype(BF16), wd_ref[0])


def _ffn_kernel(x_ref, g_ref, wu_ref, wd_ref, o_ref, xn_scr):
    _ffn_steps(lambda: x_ref[...], g_ref, wu_ref, wd_ref, o_ref, xn_scr)


def _merge_ffn_kernel(x_ref, oa_ref, ob_ref, wa_ref, wb_ref, g_ref, wu_ref, wd_ref, o_ref, xn_scr):
    def merged():
        return x_ref[...] + _mm(oa_ref[...], wa_ref[0]) + _mm(ob_ref[...], wb_ref[0])

    _ffn_steps(merged, g_ref, wu_ref, wd_ref, o_ref, xn_scr)


def _ffn(x, g, w_up, w_down, l, merge=None, *, tm=1024, tf=1024):
    rows, d = x.shape
    dff = w_up.shape[2]
    tm = _row_tile(rows, tm)
    if merge is None:
        tf = _row_tile(dff, 2 * tf)
    row = lambda w: pl.BlockSpec((tm, w), lambda i, j: (i, 0))
    ffn_specs = [_layer_spec(g, l), pl.BlockSpec((1, d, tf), lambda i, j: (l, 0, j)),
                 pl.BlockSpec((1, tf, d), lambda i, j: (l, j, 0))]
    if merge is None:
        body, args, specs = _ffn_kernel, (x,), [row(d)]
    else:
        oa, ob, w_out, li = merge
        body, args = _merge_ffn_kernel, (x, oa, ob, w_out, w_out)
        specs = [row(d), row(WA), row(VB),
                 pl.BlockSpec((1, WA, d), lambda i, j: (li, 0, 0)),
                 pl.BlockSpec((1, VB, d), lambda i, j: (li, WA // VB, 0))]
    return pl.pallas_call(
        body,
        grid=(rows // tm, dff // tf),
        in_specs=specs + ffn_specs,
        out_specs=row(d),
        out_shape=jax.ShapeDtypeStruct((rows, d), F32),
        scratch_shapes=[pltpu.VMEM((tm, d), BF16)],
        compiler_params=_params("parallel", "arbitrary"),
        name="ffn" if merge is None else "merge_ffn",
    )(*args, g, w_up, w_down)


def _inproj_kernel(x_ref, g_ref, w_ref, wlr_ref, gw_ref, gbias_ref, qg_ref, kg_ref, bd_ref, *outs,
                   tail_every):
    if tail_every:
        q_o, k_o, v_o, kt_o, vt_o, qb_o, kb_o, la_o, vb_o, gb_o = outs
    else:
        q_o, k_o, v_o, qb_o, kb_o, la_o, vb_o, gb_o = outs
    xn = _rms(x_ref[...], g_ref[0]).astype(BF16)

    def seg(a, b):
        return _mm(xn, w_ref[0, :, a:b])

    def head_norm(h, gain):
        sq = (h * h).astype(BF16)
        w = bd_ref.shape[0]
        ss = jnp.concatenate([_mm(sq[:, c:c + w], bd_ref[...]) for c in range(0, WA, w)], axis=1)
        return h * lax.rsqrt(ss * (1.0 / HEAD_DIM_A) + EPS) * gain

    q_o[...] = (head_norm(seg(0, WA), qg_ref[0]) * (HEAD_DIM_A ** -0.5)).astype(BF16)
    k = head_norm(seg(WA, 2 * WA), kg_ref[0])
    v = seg(2 * WA, 3 * WA)
    k_o[...] = k.astype(k_o.dtype)
    v_o[...] = v.astype(v_o.dtype)
    if tail_every:
        @pl.when(pl.program_id(0) % tail_every == tail_every - 1)
        def _():
            kt_o[0] = k.T
            vt_o[0] = v.T
    o = 3 * WA
    hb = seg(o, o + 2 * KB)
    qb_o[...] = hb[:, :KB] * (HEAD_K_B ** -0.5)
    kb_o[...] = hb[:, KB:]
    vb_o[...] = seg(o + 2 * KB, o + 2 * KB + VB).astype(BF16)
    gb_o[...] = seg(o + 2 * KB + VB, P_MAIN)
    lr = _mm(xn, wlr_ref[0])
    logit = _mm(lr.astype(BF16), gw_ref[0]) + gbias_ref[0]
    log_sig = jnp.minimum(logit, 0.0) - jnp.log1p(jnp.exp(-jnp.abs(logit)))
    la_o[...] = log_sig * (1.0 / GATE_NORM)


def _inproj(x, g, l, w_in, w_lr, gate_w, gate_b, qg, kg, i, bd, *, stream_len=None, tm=512):
    rows, d = x.shape
    tm = _row_tile(rows, tm)
    row = lambda w: pl.BlockSpec((tm, w), lambda r: (r, 0))
    kv_dtype = BF16 if stream_len else F32
    outs = [(WA, BF16), (WA, kv_dtype), (WA, kv_dtype), (KB, F32), (KB, F32), (KB, F32), (VB, BF16), (VB, F32)]
    out_specs = [row(w) for w, _ in outs]
    out_shape = [jax.ShapeDtypeStruct((rows, w), dt) for w, dt in outs]
    tail_every = None
    if stream_len:
        assert stream_len % tm == 0 and tm == PAD
        tail_every = stream_len // tm
        tail_spec = pl.BlockSpec((1, WA, tm), lambda r: (r // tail_every, 0, 0))
        tail_shape = jax.ShapeDtypeStruct((rows // stream_len, WA, tm), F32)
        out_specs[3:3] = [tail_spec, tail_spec]
        out_shape[3:3] = [tail_shape, tail_shape]
    return pl.pallas_call(
        functools.partial(_inproj_kernel, tail_every=tail_every),
        grid=(rows // tm,),
        in_specs=[row(d), _layer_spec(g, l), _layer_spec(w_in, i), _layer_spec(w_lr, i),
                  _layer_spec(gate_w, i), _layer_spec(gate_b, i), _layer_spec(qg, i),
                  _layer_spec(kg, i), _full_spec(bd)],
        out_specs=out_specs,
        out_shape=out_shape,
        compiler_params=_params("arbitrary"),
        name="inproj",
    )(x, g, w_in, w_lr, gate_w, gate_b, qg, kg, bd)


def _softmax_pv(scores, pv_fns):
    m = scores[0].max(axis=-1, keepdims=True)
    for s in scores[1:]:
        m = jnp.maximum(m, s.max(axis=-1, keepdims=True))
    l = 0.0
    pv = 0.0
    for s, fn in zip(scores, pv_fns):
        p = jnp.exp(s - m)
        l = l + p.sum(axis=-1, keepdims=True)
        pv = pv + fn(p.astype(BF16))
    return pv / l


def _stack_pair(q2, first_half):
    zero = jnp.zeros_like(q2)
    return jnp.concatenate([jnp.where(first_half, q2, zero), jnp.where(first_half, zero, q2)], axis=0)


GROUP = 4
GROUP_ROWS = GROUP * CHUNK
WINDOW = PAD + GROUP_ROWS


def _attn_prompt_kernel(q_ref, k_ref, v_ref, bias_ref, o_ref, *, ng):
    first_half = lax.broadcasted_iota(jnp.int32, (GROUP_ROWS, LANES), 1) < HEAD_DIM_A

    def group(r0, keys, bias_cols):
        for p in range(N_HEADS_A // 2):
            lanes = slice(p * LANES, (p + 1) * LANES)
            kw = k_ref[keys, lanes]
            vw = v_ref[keys, lanes]
            qs = _stack_pair(q_ref[pl.ds(r0, GROUP_ROWS), lanes], first_half)
            s = _mm_nt(qs, kw) + bias_ref[p, :, bias_cols]
            o = _softmax_pv([s], [lambda pb: _mm(pb, vw)])
            o_ref[pl.ds(r0, GROUP_ROWS), lanes] = jnp.where(
                first_half, o[:GROUP_ROWS], o[GROUP_ROWS:]).astype(BF16)

    n_ramp = min(ng, PAD // GROUP_ROWS)
    for g in range(n_ramp):
        n_keys = (g + 1) * GROUP_ROWS
        group(g * GROUP_ROWS, slice(0, n_keys), slice(WINDOW - n_keys, WINDOW))

    def full_group(g, carry):
        r0 = pl.multiple_of(g * GROUP_ROWS, GROUP_ROWS)
        group(r0, pl.ds(pl.multiple_of(r0 - PAD, GROUP_ROWS), WINDOW), slice(0, WINDOW))
        return carry

    lax.fori_loop(n_ramp, ng, full_group, 0, unroll=2)


def _attn_prompt(q, k, v, bias, nb, t):
    ng = t // GROUP_ROWS
    blk = pl.BlockSpec((t, WA), lambda b: (b, 0))
    return pl.pallas_call(
        functools.partial(_attn_prompt_kernel, ng=ng),
        grid=(nb,),
        in_specs=[blk, blk, blk, _full_spec(bias)],
        out_specs=blk,
        out_shape=jax.ShapeDtypeStruct((nb * t, WA), BF16),
        compiler_params=_params("parallel"),
        name="attn_prompt",
    )(q, k, v, bias)


def _attn_sample_kernel(q_ref, kn_ref, vn_ref, ckt_ref, cvt_ref, bias_c_ref, bias_n_ref, o_ref, *, ns, t):
    first_half = lax.broadcasted_iota(jnp.int32, (t, LANES), 1) < HEAD_DIM_A
    for s in range(ns):
        rows = slice(s * t, (s + 1) * t)
        for p in range(N_HEADS_A // 2):
            lanes = slice(p * LANES, (p + 1) * LANES)
            qs = _stack_pair(q_ref[rows, lanes], first_half)
            kct = ckt_ref[0, s, lanes, :].astype(BF16)
            vct = cvt_ref[0, s, lanes, :].astype(BF16)
            kn = kn_ref[rows, lanes].astype(BF16)
            vn = vn_ref[rows, lanes].astype(BF16)
            s_c = _mm(qs, kct) + bias_c_ref[p]
            s_n = _mm_nt(qs, kn) + bias_n_ref[p]
            o = _softmax_pv([s_c, s_n], [lambda pb: _mm_nt(pb, vct), lambda pb: _mm(pb, vn)])
            o_ref[rows, lanes] = jnp.where(first_half, o[:t], o[t:]).astype(BF16)


def _attn_sample(q, k_new, v_new, cache_kt, cache_vt, li, bias_c, bias_n, nb, t, *, ns=4):
    lc = cache_kt.shape[3]
    assert nb % ns == 0
    blk = pl.BlockSpec((ns * t, WA), lambda b: (b, 0))
    cblk = pl.BlockSpec((1, ns, WA, lc), lambda b: (li, b, 0, 0))
    return pl.pallas_call(
        functools.partial(_attn_sample_kernel, ns=ns, t=t),
        grid=(nb // ns,),
        in_specs=[blk, blk, blk, cblk, cblk, _full_spec(bias_c), _full_spec(bias_n)],
        out_specs=blk,
        out_shape=jax.ShapeDtypeStruct((nb * t, WA), BF16),
        compiler_params=_params("parallel"),
        name="attn_sample",
    )(q, k_new, v_new, cache_kt, cache_vt, bias_c, bias_n)


def _gla_kernel(q_ref, k_ref, v_ref, la_ref, gb_ref, og_ref, s0_ref, o_ref, sout_ref, st_scr,
                *, ns, nblk, ncb, c_len):
    n_pairs = N_HEADS_B // 2
    r_len = ncb * c_len

    @pl.when(pl.program_id(1) == 0)
    def _():
        for s in range(ns):
            for p in range(n_pairs):
                st_scr[s, p] = s0_ref[s, p * LANES:(p + 1) * LANES, :].T

    ri = lax.broadcasted_iota(jnp.int32, (r_len, r_len), 0)
    ci = lax.broadcasted_iota(jnp.int32, (r_len, r_len), 1)
    same_chunk = (ri // c_len) == (ci // c_len)
    causal = same_chunk & (ri >= ci)
    causal2 = jnp.concatenate([causal, causal], axis=0)
    tril = causal.astype(BF16)
    ones_bd = same_chunk.astype(BF16)
    row_chunk = lax.broadcasted_iota(jnp.int32, (r_len, HEAD_V_B), 0) // c_len
    first_half = lax.broadcasted_iota(jnp.int32, (r_len, LANES), 1) < HEAD_K_B
    first_half_sq = lax.broadcasted_iota(jnp.int32, (LANES, LANES), 1) < HEAD_K_B

    def one_block(s, r0):
        rows = pl.ds(r0, r_len)
        la = la_ref[s, rows, :]
        la_hi = la.astype(BF16)
        la_lo = (la - la_hi.astype(F32)).astype(BF16)
        b = _mm(tril, la_hi) + _mm(tril, la_lo)
        bl = _mm(ones_bd, la_hi) + _mm(ones_bd, la_lo)
        ref = 0.5 * bl
        q = q_ref[s, rows, :]
        k = k_ref[s, rows, :]
        qt = (q * jnp.exp(b - ref)).astype(BF16)
        kt = (k * jnp.exp(ref - b)).astype(BF16)
        qd = (q * jnp.exp(b)).astype(BF16)
        kd = (k * jnp.exp(bl - b)).astype(BF16)
        dl = jnp.exp(bl)
        for p in range(n_pairs):
            lanes = slice(p * LANES, (p + 1) * LANES)
            kt2 = kt[:, lanes]
            kd2 = kd[:, lanes]
            vlanes = [slice((2 * p + e) * HEAD_V_B, (2 * p + e + 1) * HEAD_V_B) for e in range(2)]
            vs = [v_ref[s, rows, vl] for vl in vlanes]
            a = jnp.where(causal2, _mm_nt(_stack_pair(qt[:, lanes], first_half), kt2), 0.0).astype(BF16)
            intra = [_mm(a[e * r_len:(e + 1) * r_len], vs[e]) for e in range(2)]
            qd_stack = _stack_pair(qd[:, lanes], first_half)
            v_bd = jnp.concatenate([jnp.where(row_chunk == c, vs[e], jnp.zeros_like(vs[e]))
                                    for e in range(2) for c in range(ncb)], axis=1)
            upd = _mm_tn(v_bd, kd2)
            st = st_scr[s, p]
            inter = [[], []]
            for c in range(ncb):
                q_c = jnp.concatenate([qd_stack[e * r_len + c * c_len:e * r_len + (c + 1) * c_len] for e in range(2)], axis=0)
                o_c = _mm_nt(q_c, st.astype(BF16))
                inter[0].append(o_c[:c_len])
                inter[1].append(o_c[c_len:])
                u0 = upd[c * HEAD_V_B:(c + 1) * HEAD_V_B]
                u1 = upd[(ncb + c) * HEAD_V_B:(ncb + c + 1) * HEAD_V_B]
                st = st * dl[c * c_len:c * c_len + 1, lanes] + jnp.where(first_half_sq, u0, u1)
            st_scr[s, p] = st
            for e in range(2):
                o_inter = inter[e][0] if ncb == 1 else jnp.concatenate(inter[e], axis=0)
                gate = gb_ref[s, rows, vlanes[e]]
                o = _rms(intra[e] + o_inter, og_ref[0]) * (gate * _sigmoid(gate))
                o_ref[s, rows, vlanes[e]] = o.astype(BF16)

    def block(j, carry):
        for s in range(ns):
            one_block(s, pl.multiple_of(j * r_len, r_len))
        return carry

    lax.fori_loop(0, nblk, block, 0)

    @pl.when(pl.program_id(1) == pl.num_programs(1) - 1)
    def _():
        for s in range(ns):
            for p in range(n_pairs):
                sout_ref[s, p * LANES:(p + 1) * LANES, :] = st_scr[s, p].T


def _gla(q, k, v, la, gb, og, li, s0, nb, t, c_len, *, ns, tb=512, r_len=256):
    tb = _row_tile(t, tb)
    r_len = _row_tile(tb, r_len)
    assert nb % ns == 0 and r_len % c_len == 0
    kblk = pl.BlockSpec((ns, tb, KB), lambda b, j: (b, j, 0))
    vblk = pl.BlockSpec((ns, tb, VB), lambda b, j: (b, j, 0))
    sblk = pl.BlockSpec((ns, KB, HEAD_V_B), lambda b, j: (b, 0, 0))
    per_stream = lambda a: a.reshape(nb, t, a.shape[-1])
    o, s_out = pl.pallas_call(
        functools.partial(_gla_kernel, ns=ns, nblk=tb // r_len, ncb=r_len // c_len, c_len=c_len),
        grid=(nb // ns, t // tb),
        in_specs=[kblk, kblk, vblk, kblk, vblk, _layer_spec(og, li), sblk],
        out_specs=[vblk, sblk],
        out_shape=[jax.ShapeDtypeStruct((nb, t, VB), BF16),
                   jax.ShapeDtypeStruct((nb, KB, HEAD_V_B), F32)],
        scratch_shapes=[pltpu.VMEM((ns, N_HEADS_B // 2, LANES, LANES), F32)],
        compiler_params=_params("parallel", "arbitrary"),
        name="gla",
    )(per_stream(q), per_stream(k), per_stream(v), per_stream(la), per_stream(gb), og, s0)
    return o.reshape(nb * t, VB), s_out


def _conv_kernel(x_ref, init_ref, g_ref, w1_ref, b1_ref, wdw_ref, bdw_ref, lng_ref, lnb_ref,
                 w2_ref, b2_ref, o_ref, tail_ref, buf_scr, *, ns, tt):
    d = x_ref.shape[1]
    lo = CONV_HALO - (CONV_W - 1)

    @pl.when(pl.program_id(1) == 0)
    def _():
        for s in range(ns):
            buf_scr[s, lo:CONV_HALO, :] = init_ref[s]
            buf_scr[s, CONV_HALO + tt:CONV_HALO + tt + SUBLANES, :] = jnp.zeros((SUBLANES, d), F32)

    x = x_ref[...]
    u = _mm(_rms(x, g_ref[0]).astype(BF16), w1_ref[0]) + b1_ref[0]
    glu = u[:, :d] * _sigmoid(u[:, d:])
    ys = []
    for s in range(ns):
        buf_scr[s, CONV_HALO:CONV_HALO + tt, :] = glu[s * tt:(s + 1) * tt]
        y = bdw_ref[0]
        for r in range(SUBLANES):
            part = None
            for k in range(lo, lo + CONV_W):
                if k % SUBLANES == r:
                    term = wdw_ref[0, k - lo:k - lo + 1, :] * buf_scr[s, k - r:k - r + tt + SUBLANES, :]
                    part = term if part is None else part + term
            y = y + part[r:r + tt, :]
        ys.append(y)
        tail = buf_scr[s, tt:tt + CONV_HALO, :]
        tail_ref[s] = tail
        buf_scr[s, 0:CONV_HALO, :] = tail
    y = ys[0] if ns == 1 else jnp.concatenate(ys, axis=0)
    mu = jnp.mean(y, axis=-1, keepdims=True)
    yc = y - mu
    var = jnp.mean(yc * yc, axis=-1, keepdims=True)
    yn = yc * lax.rsqrt(var + EPS) * lng_ref[0] + lnb_ref[0]
    z = (yn * _sigmoid(yn)).astype(BF16)
    o_ref[...] = x + _mm(z, w2_ref[0]) + b2_ref[0]


def _conv(x, init, g, l, conv_params, j, nb, t, *, tt=512, ns=8):
    rows, d = x.shape
    tt = _row_tile(t, tt)
    nt = t // tt
    ns = ns if nt == 1 else 1
    assert nb % ns == 0
    xblk = pl.BlockSpec((ns * tt, d), lambda b, i: (b * nt + i, 0))
    return pl.pallas_call(
        functools.partial(_conv_kernel, ns=ns, tt=tt),
        grid=(nb // ns, nt),
        in_specs=[xblk, pl.BlockSpec((ns, CONV_W - 1, d), lambda b, i: (b, 0, 0)), _layer_spec(g, l)]
                 + [_layer_spec(a, j) for a in conv_params],
        out_specs=[xblk, pl.BlockSpec((ns, CONV_HALO, d), lambda b, i: (b, 0, 0))],
        out_shape=[jax.ShapeDtypeStruct((rows, d), F32),
                   jax.ShapeDtypeStruct((nb, CONV_HALO, d), F32)],
        scratch_shapes=[pltpu.VMEM((ns, CONV_HALO + tt + SUBLANES, d), F32)],
        compiler_params=_params("parallel", "arbitrary"),
        name="conv",
    )(x, init, g, *conv_params)


def _band_bias(table):
    n_far = PAD + CHUNK - REL_MAX
    ext = jnp.concatenate([jnp.repeat(table[..., -1:], n_far, axis=-1), table[..., -2::-1]], axis=-1)
    period = ext.shape[-1] + 1
    ext = jnp.pad(ext, ((0, 0), (0, 0), (0, 1)))
    wrapped = jnp.tile(ext, (1, 1, CHUNK + 1))[..., :CHUNK * (period + 1)]
    wrapped = wrapped.reshape(ext.shape[:2] + (CHUNK, period + 1))
    return wrapped[:, :, ::-1, :BAND]


def _group_bias(bias):
    per_chunk = [jnp.pad(bias, ((0, 0), (0, 0), (c * CHUNK, (GROUP - 1 - c) * CHUNK)), constant_values=-1e30)
                 for c in range(GROUP)]
    stacked = jnp.stack(per_chunk, axis=1)
    return stacked.reshape(N_HEADS_A // 2, 2 * GROUP_ROWS, WINDOW)


def _positions_minor(a):
    n_l, n_b, n_t = a.shape[:3]
    return jnp.transpose(a, (0, 1, 3, 4, 2)).reshape(n_l, n_b, WA, n_t)


def _heads_minor(a):
    n_l, n_b, _, n_t = a.shape
    return jnp.transpose(a.reshape(n_l, n_b, N_HEADS_A, HEAD_DIM_A, n_t), (0, 1, 4, 2, 3))


def kernel(x_prompt, x_sample, cache_a_k, cache_a_v, state_b, cache_c, norm_mix_g, norm_ffn_g, w_in_ab, a_q_norm_g, a_k_norm_g, a_rel_bias, b_gate_w_up, b_gate_b, b_out_norm_g, w_out_ab, c_w_pw1, c_b_pw1, c_w_dw, c_b_dw, c_ln_g, c_ln_b, c_w_pw2, c_b_pw2, w_ffn_up, w_ffn_down):
    nbp, tp, d = x_prompt.shape
    nbs, ts, _ = x_sample.shape
    depth = norm_mix_g.shape[0]
    lp = min(PAD, tp)
    lc = cache_a_k.shape[2]
    assert lc == PAD and lp == PAD and ts <= CHUNK and tp % CHUNK == 0
    xp = x_prompt.reshape(nbp * tp, d)
    xs = x_sample.reshape(nbs * ts, d)
    vec = lambda a: a.reshape(a.shape[0], 1, a.shape[1])

    head_id = np.arange(MXU_TILE) // HEAD_DIM_A
    bd = jnp.asarray(head_id[:, None] == head_id[None, :], BF16)
    g_mix, g_ffn = vec(norm_mix_g), vec(norm_ffn_g)
    w_up, w_down = w_ffn_up.astype(BF16), w_ffn_down.astype(BF16)
    w_in = w_in_ab.astype(BF16)
    w_lr = jnp.pad(w_in_ab[:, :, P_MAIN:], ((0, 0), (0, 0), (0, LANES - GATE_RANK))).astype(BF16)
    gate_w = jnp.pad(b_gate_w_up, ((0, 0), (0, LANES - GATE_RANK), (0, 0))).astype(BF16)
    proj = (w_in, w_lr, gate_w, vec(b_gate_b),
            vec(jnp.tile(a_q_norm_g, (1, N_HEADS_A))), vec(jnp.tile(a_k_norm_g, (1, N_HEADS_A))))
    og = vec(b_out_norm_g)
    w_out = w_out_ab.astype(BF16)
    conv_params = (c_w_pw1.astype(BF16), vec(c_b_pw1), c_w_dw, vec(c_b_dw), vec(c_ln_g), vec(c_ln_b),
                   c_w_pw2.astype(BF16), vec(c_b_pw2))
    bias = _band_bias(a_rel_bias)
    cache_kt, cache_vt = _positions_minor(cache_a_k), _positions_minor(cache_a_v)
    pair_rows = lambda a: a.reshape(N_HEADS_A // 2, 2 * ts, a.shape[-1])

    akp, avp, aks, avs, sbp, sbs, ccp, ccs = [], [], [], [], [], [], [], []
    for l in range(depth):
        if l % 2 == 0:
            i = l // 2
            q, k, v, kt, vt, qb, kb, la, vb, gb = _inproj(xp, g_mix, l, *proj, i, bd, stream_len=tp)
            oa = _attn_prompt(q, k, v, _group_bias(bias[i]), nbp, tp)
            ob, sb = _gla(qb, kb, vb, la, gb, og, i, jnp.zeros((nbp, KB, HEAD_V_B), F32), nbp, tp, CHUNK,
                          ns=min(nbp, 2))
            akp.append(kt)
            avp.append(vt)
            sbp.append(sb.reshape(nbp, N_HEADS_B, HEAD_K_B, HEAD_V_B))
            merge_p = (oa, ob, w_out, i)

            q, k, v, qb, kb, la, vb, gb = _inproj(xs, g_mix, l, *proj, i, bd)
            oa = _attn_sample(q, k, v, cache_kt, cache_vt, i,
                              pair_rows(bias[i, :, :ts, :lc]), pair_rows(bias[i, :, :ts, lc:lc + ts]), nbs, ts,
                              ns=min(nbs, 4))
            ob, sb = _gla(qb, kb, vb, la, gb, og, i, state_b[i].reshape(nbs, KB, HEAD_V_B), nbs, ts, ts,
                          ns=min(nbs, 8))
            aks.append(k.reshape(nbs, ts, N_HEADS_A, HEAD_DIM_A))
            avs.append(v.reshape(nbs, ts, N_HEADS_A, HEAD_DIM_A))
            sbs.append(sb.reshape(nbs, N_HEADS_B, HEAD_K_B, HEAD_V_B))
            merge_s = (oa, ob, w_out, i)
        else:
            merge_p = merge_s = None
            j = l // 2
            xp, tail = _conv(xp, jnp.zeros((nbp, CONV_W - 1, d), F32), g_mix, l, conv_params, j, nbp, tp)
            ccp.append(tail[:, CONV_HALO - (CONV_W - 1):])
            xs, tail = _conv(xs, cache_c[j], g_mix, l, conv_params, j, nbs, ts, ns=min(nbs, 8))
            ccs.append(tail[:, CONV_HALO - (CONV_W - 1):])
        xp = _ffn(xp, g_ffn, w_up, w_down, l, merge_p)
        xs = _ffn(xs, g_ffn, w_up, w_down, l, merge_s)
    return (xp.reshape(nbp, tp, d), xs.reshape(nbs, ts, d),
            _heads_minor(jnp.stack(akp)), _heads_minor(jnp.stack(avp)), jnp.stack(aks), jnp.stack(avs),
            jnp.stack(sbp), jnp.stack(sbs), jnp.stack(ccp), jnp.stack(ccs))
```

```python
import functools

import numpy as np
import jax
import jax.numpy as jnp
from jax import lax
from jax.experimental import pallas as pl
from jax.experimental.pallas import tpu as pltpu

F32 = jnp.float32
BF16 = jnp.bfloat16

EPS = 1e-6
CHUNK = 64
PAST_CHUNKS = 8
PAD = PAST_CHUNKS * CHUNK
BAND = PAD + CHUNK
N_HEADS_A = 8
HEAD_DIM_A = 64
REL_MIN = -(CHUNK - 1)
REL_MAX = 2 * CHUNK
N_HEADS_B = 4
HEAD_K_B = 64
HEAD_V_B = 128
GATE_RANK = 16
GATE_NORM = 16.0
CONV_W = 31
CONV_HALO = 32
WA = N_HEADS_A * HEAD_DIM_A
KB = N_HEADS_B * HEAD_K_B
VB = N_HEADS_B * HEAD_V_B
P_MAIN = 3 * WA + 2 * KB + 2 * VB
LANES = 128
SUBLANES = 8
MXU_TILE = 256
VMEM_LIMIT = 56 * 1024 * 1024


def _mm(a, b):
    return jnp.dot(a, b, preferred_element_type=F32)


def _mm_nt(a, b):
    return lax.dot_general(a, b, (((1,), (1,)), ((), ())), preferred_element_type=F32)


def _mm_tn(a, b):
    return lax.dot_general(a, b, (((0,), (0,)), ((), ())), preferred_element_type=F32)


def _rms(x, g):
    ms = jnp.mean(x * x, axis=-1, keepdims=True)
    return x * lax.rsqrt(ms + EPS) * g


def _sigmoid(x):
    return 1.0 / (1.0 + jnp.exp(-x))


def _params(*sem):
    return pltpu.CompilerParams(dimension_semantics=sem, vmem_limit_bytes=VMEM_LIMIT)


def _row_tile(rows, want):
    tm = min(rows, want)
    assert rows % tm == 0, (rows, tm)
    return tm


def _layer_spec(a, l):
    return pl.BlockSpec((1,) + a.shape[1:], lambda *_: (l,) + (0,) * (a.ndim - 1))


def _full_spec(a):
    return pl.BlockSpec(a.shape, lambda *_: (0,) * a.ndim)


def _ffn_steps(x1, g_ref, wu_ref, wd_ref, o_ref, xn_scr, n_sub):
    @pl.when(pl.program_id(1) == 0)
    def _():
        x = x1()
        o_ref[...] = x
        xn_scr[...] = _rms(x, g_ref[0]).astype(BF16)

    sub = wu_ref.shape[2] // n_sub
    for c in range(n_sub):
        h = jnp.maximum(_mm(xn_scr[...], wu_ref[0, :, c * sub:(c + 1) * sub]), 0.0)
        o_ref[...] += _mm((h * h).astype(BF16), wd_ref[0, c * sub:(c + 1) * sub, :])


def _ffn_kernel(x_ref, g_ref, wu_ref, wd_ref, o_ref, xn_scr, *, n_sub):
    _ffn_steps(lambda: x_ref[...], g_ref, wu_ref, wd_ref, o_ref, xn_scr, n_sub)


def _merge_ffn_kernel(x_ref, oa_ref, ob_ref, wa_ref, wb_ref, g_ref, wu_ref, wd_ref, o_ref, xn_scr,
                      *, n_sub):
    def merged():
        return x_ref[...] + _mm(oa_ref[...], wa_ref[0]) + _mm(ob_ref[...], wb_ref[0])

    _ffn_steps(merged, g_ref, wu_ref, wd_ref, o_ref, xn_scr, n_sub)


def _ffn(x, g, w_up, w_down, l, merge=None, *, tm=1024, tf=2048):
    rows, d = x.shape
    dff = w_up.shape[2]
    tm = _row_tile(rows, tm)
    tf = _row_tile(dff, tf)
    n_sub = 1 if merge is None else 2
    row = lambda w: pl.BlockSpec((tm, w), lambda i, j: (i, 0))
    ffn_specs = [_layer_spec(g, l), pl.BlockSpec((1, d, tf), lambda i, j: (l, 0, j)),
                 pl.BlockSpec((1, tf, d), lambda i, j: (l, j, 0))]
    if merge is None:
        body, args, specs = _ffn_kernel, (x,), [row(d)]
    else:
        oa, ob, w_out, li = merge
        body, args = _merge_ffn_kernel, (x, oa, ob, w_out, w_out)
        specs = [row(d), row(WA), row(VB),
                 pl.BlockSpec((1, WA, d), lambda i, j: (li, 0, 0)),
                 pl.BlockSpec((1, VB, d), lambda i, j: (li, WA // VB, 0))]
    return pl.pallas_call(
        functools.partial(body, n_sub=n_sub),
        grid=(rows // tm, dff // tf),
        in_specs=specs + ffn_specs,
        out_specs=row(d),
        out_shape=jax.ShapeDtypeStruct((rows, d), F32),
        scratch_shapes=[pltpu.VMEM((tm, d), BF16)],
        compiler_params=_params("parallel", "arbitrary"),
        name="ffn" if merge is None else "merge_ffn",
    )(*args, g, w_up, w_down)


def _inproj_kernel(x_ref, g_ref, w_ref, wlr_ref, gw_ref, gbias_ref, qg_ref, kg_ref, bd_ref, *outs,
                   tail_every):
    if tail_every:
        q_o, k_o, v_o, kt_o, vt_o, qb_o, kb_o, la_o, vb_o, gb_o = outs
    else:
        q_o, k_o, v_o, qb_o, kb_o, la_o, vb_o, gb_o = outs
    xn = _rms(x_ref[...], g_ref[0]).astype(BF16)

    def seg(a, b):
        return _mm(xn, w_ref[0, :, a:b])

    def head_norm(h, gain):
        sq = (h * h).astype(BF16)
        w = bd_ref.shape[0]
        ss = jnp.concatenate([_mm(sq[:, c:c + w], bd_ref[...]) for c in range(0, WA, w)], axis=1)
        return h * lax.rsqrt(ss * (1.0 / HEAD_DIM_A) + EPS) * gain

    q_o[...] = (head_norm(seg(0, WA), qg_ref[0]) * (HEAD_DIM_A ** -0.5)).astype(BF16)
    k = head_norm(seg(WA, 2 * WA), kg_ref[0])
    v = seg(2 * WA, 3 * WA)
    k_o[...] = k.astype(k_o.dtype)
    v_o[...] = v.astype(v_o.dtype)
    if tail_every:
        @pl.when(pl.program_id(0) % tail_every == tail_every - 1)
        def _():
            kt_o[0] = k.T
            vt_o[0] = v.T
    o = 3 * WA
    hb = seg(o, o + 2 * KB)
    qb_o[...] = hb[:, :KB] * (HEAD_K_B ** -0.5)
    kb_o[...] = hb[:, KB:]
    vb_o[...] = seg(o + 2 * KB, o + 2 * KB + VB).astype(BF16)
    gb_o[...] = seg(o + 2 * KB + VB, P_MAIN)
    lr = _mm(xn, wlr_ref[0])
    logit = _mm(lr.astype(BF16), gw_ref[0]) + gbias_ref[0]
    log_sig = jnp.minimum(logit, 0.0) - jnp.log1p(jnp.exp(-jnp.abs(logit)))
    la_o[...] = log_sig * (1.0 / GATE_NORM)


def _inproj(x, g, l, w_in, w_lr, gate_w, gate_b, qg, kg, i, bd, *, stream_len=None, tm=512):
    rows, d = x.shape
    tm = _row_tile(rows, tm)
    row = lambda w: pl.BlockSpec((tm, w), lambda r: (r, 0))
    kv_dtype = BF16 if stream_len else F32
    outs = [(WA, BF16), (WA, kv_dtype), (WA, kv_dtype), (KB, F32), (KB, F32), (KB, F32), (VB, BF16), (VB, F32)]
    out_specs = [row(w) for w, _ in outs]
    out_shape = [jax.ShapeDtypeStruct((rows, w), dt) for w, dt in outs]
    tail_every = None
    if stream_len:
        assert stream_len % tm == 0 and tm == PAD
        tail_every = stream_len // tm
        tail_spec = pl.BlockSpec((1, WA, tm), lambda r: (r // tail_every, 0, 0))
        tail_shape = jax.ShapeDtypeStruct((rows // stream_len, WA, tm), F32)
        out_specs[3:3] = [tail_spec, tail_spec]
        out_shape[3:3] = [tail_shape, tail_shape]
    return pl.pallas_call(
        functools.partial(_inproj_kernel, tail_every=tail_every),
        grid=(rows // tm,),
        in_specs=[row(d), _layer_spec(g, l), _layer_spec(w_in, i), _layer_spec(w_lr, i),
                  _layer_spec(gate_w, i), _layer_spec(gate_b, i), _layer_spec(qg, i),
                  _layer_spec(kg, i), _full_spec(bd)],
        out_specs=out_specs,
        out_shape=out_shape,
        compiler_params=_params("arbitrary"),
        name="inproj",
    )(x, g, w_in, w_lr, gate_w, gate_b, qg, kg, bd)


def _softmax_pv(scores, pv_fns):
    m = scores[0].max(axis=-1, keepdims=True)
    for s in scores[1:]:
        m = jnp.maximum(m, s.max(axis=-1, keepdims=True))
    l = 0.0
    pv = 0.0
    for s, fn in zip(scores, pv_fns):
        p = jnp.exp(s - m)
        l = l + p.sum(axis=-1, keepdims=True)
        pv = pv + fn(p.astype(BF16))
    return pv / l


def _stack_pair(q2, first_half):
    zero = jnp.zeros_like(q2)
    return jnp.concatenate([jnp.where(first_half, q2, zero), jnp.where(first_half, zero, q2)], axis=0)


GROUP = 4
GROUP_ROWS = GROUP * CHUNK
WINDOW = PAD + GROUP_ROWS


def _attn_prompt_kernel(q_ref, k_ref, v_ref, bias_ref, o_ref, *, ng):
    first_half = lax.broadcasted_iota(jnp.int32, (GROUP_ROWS, LANES), 1) < HEAD_DIM_A

    def group(r0, keys, bias_cols):
        for p in range(N_HEADS_A // 2):
            lanes = slice(p * LANES, (p + 1) * LANES)
            kw = k_ref[keys, lanes]
            vw = v_ref[keys, lanes]
            qs = _stack_pair(q_ref[pl.ds(r0, GROUP_ROWS), lanes], first_half)
            s = _mm_nt(qs, kw) + bias_ref[p, :, bias_cols]
            o = _softmax_pv([s], [lambda pb: _mm(pb, vw)])
            o_ref[pl.ds(r0, GROUP_ROWS), lanes] = jnp.where(
                first_half, o[:GROUP_ROWS], o[GROUP_ROWS:]).astype(BF16)

    n_ramp = min(ng, PAD // GROUP_ROWS)
    for g in range(n_ramp):
        n_keys = (g + 1) * GROUP_ROWS
        group(g * GROUP_ROWS, slice(0, n_keys), slice(WINDOW - n_keys, WINDOW))

    def full_group(g, carry):
        r0 = pl.multiple_of(g * GROUP_ROWS, GROUP_ROWS)
        group(r0, pl.ds(pl.multiple_of(r0 - PAD, GROUP_ROWS), WINDOW), slice(0, WINDOW))
        return carry

    lax.fori_loop(n_ramp, ng, full_group, 0, unroll=2)


def _attn_prompt(q, k, v, bias, nb, t):
    ng = t // GROUP_ROWS
    blk = pl.BlockSpec((t, WA), lambda b: (b, 0))
    return pl.pallas_call(
        functools.partial(_attn_prompt_kernel, ng=ng),
        grid=(nb,),
        in_specs=[blk, blk, blk, _full_spec(bias)],
        out_specs=blk,
        out_shape=jax.ShapeDtypeStruct((nb * t, WA), BF16),
        compiler_params=_params("parallel"),
        name="attn_prompt",
    )(q, k, v, bias)


def _attn_sample_kernel(q_ref, kn_ref, vn_ref, ckt_ref, cvt_ref, bias_c_ref, bias_n_ref, o_ref, *, ns, t):
    first_half = lax.broadcasted_iota(jnp.int32, (t, LANES), 1) < HEAD_DIM_A
    for s in range(ns):
        rows = slice(s * t, (s + 1) * t)
        for p in range(N_HEADS_A // 2):
            lanes = slice(p * LANES, (p + 1) * LANES)
            qs = _stack_pair(q_ref[rows, lanes], first_half)
            kct = ckt_ref[0, s, lanes, :].astype(BF16)
            vct = cvt_ref[0, s, lanes, :].astype(BF16)
            kn = kn_ref[rows, lanes].astype(BF16)
            vn = vn_ref[rows, lanes].astype(BF16)
            s_c = _mm(qs, kct) + bias_c_ref[p]
            s_n = _mm_nt(qs, kn) + bias_n_ref[p]
            o = _softmax_pv([s_c, s_n], [lambda pb: _mm_nt(pb, vct), lambda pb: _mm(pb, vn)])
            o_ref[rows, lanes] = jnp.where(first_half, o[:t], o[t:]).astype(BF16)


def _attn_sample(q, k_new, v_new, cache_kt, cache_vt, li, bias_c, bias_n, nb, t, *, ns=4):
    lc = cache_kt.shape[3]
    assert nb % ns == 0
    blk = pl.BlockSpec((ns * t, WA), lambda b: (b, 0))
    cblk = pl.BlockSpec((1, ns, WA, lc), lambda b: (li, b, 0, 0))
    return pl.pallas_call(
        functools.partial(_attn_sample_kernel, ns=ns, t=t),
        grid=(nb // ns,),
        in_specs=[blk, blk, blk, cblk, cblk, _full_spec(bias_c), _full_spec(bias_n)],
        out_specs=blk,
        out_shape=jax.ShapeDtypeStruct((nb * t, WA), BF16),
        compiler_params=_params("parallel"),
        name="attn_sample",
    )(q, k_new, v_new, cache_kt, cache_vt, bias_c, bias_n)


def _gla_kernel(q_ref, k_ref, v_ref, la_ref, gb_ref, og_ref, s0_ref, o_ref, sout_ref, st_scr,
                *, ns, nblk, ncb, c_len):
    n_pairs = N_HEADS_B // 2
    r_len = ncb * c_len

    @pl.when(pl.program_id(1) == 0)
    def _():
        for s in range(ns):
            for p in range(n_pairs):
                st_scr[s, p] = s0_ref[s, p * LANES:(p + 1) * LANES, :].T

    ri = lax.broadcasted_iota(jnp.int32, (r_len, r_len), 0)
    ci = lax.broadcasted_iota(jnp.int32, (r_len, r_len), 1)
    same_chunk = (ri // c_len) == (ci // c_len)
    causal = same_chunk & (ri >= ci)
    causal2 = jnp.concatenate([causal, causal], axis=0)
    tril = causal.astype(BF16)
    ones_bd = same_chunk.astype(BF16)
    row_chunk = lax.broadcasted_iota(jnp.int32, (r_len, HEAD_V_B), 0) // c_len
    first_half = lax.broadcasted_iota(jnp.int32, (r_len, LANES), 1) < HEAD_K_B
    first_half_sq = lax.broadcasted_iota(jnp.int32, (LANES, LANES), 1) < HEAD_K_B

    def one_block(s, r0):
        rows = pl.ds(r0, r_len)
        la = la_ref[s, rows, :]
        la_hi = la.astype(BF16)
        la_lo = (la - la_hi.astype(F32)).astype(BF16)
        b = _mm(tril, la_hi) + _mm(tril, la_lo)
        bl = _mm(ones_bd, la_hi) + _mm(ones_bd, la_lo)
        ref = 0.5 * bl
        q = q_ref[s, rows, :]
        k = k_ref[s, rows, :]
        qt = (q * jnp.exp(b - ref)).astype(BF16)
        kt = (k * jnp.exp(ref - b)).astype(BF16)
        qd = (q * jnp.exp(b)).astype(BF16)
        kd = (k * jnp.exp(bl - b)).astype(BF16)
        dl = jnp.exp(bl)
        for p in range(n_pairs):
            lanes = slice(p * LANES, (p + 1) * LANES)
            kt2 = kt[:, lanes]
            kd2 = kd[:, lanes]
            vlanes = [slice((2 * p + e) * HEAD_V_B, (2 * p + e + 1) * HEAD_V_B) for e in range(2)]
            vs = [v_ref[s, rows, vl] for vl in vlanes]
            a = jnp.where(causal2, _mm_nt(_stack_pair(qt[:, lanes], first_half), kt2), 0.0).astype(BF16)
            intra = [_mm(a[e * r_len:(e + 1) * r_len], vs[e]) for e in range(2)]
            qd_stack = _stack_pair(qd[:, lanes], first_half)
            v_bd = jnp.concatenate([jnp.where(row_chunk == c, vs[e], jnp.zeros_like(vs[e]))
                                    for e in range(2) for c in range(ncb)], axis=1)
            upd = _mm_tn(v_bd, kd2)
            st = st_scr[s, p]
            inter = [[], []]
            for c in range(ncb):
                q_c = jnp.concatenate([qd_stack[e * r_len + c * c_len:e * r_len + (c + 1) * c_len] for e in range(2)], axis=0)
                o_c = _mm_nt(q_c, st.astype(BF16))
                inter[0].append(o_c[:c_len])
                inter[1].append(o_c[c_len:])
                u0 = upd[c * HEAD_V_B:(c + 1) * HEAD_V_B]
                u1 = upd[(ncb + c) * HEAD_V_B:(ncb + c + 1) * HEAD_V_B]
                st = st * dl[c * c_len:c * c_len + 1, lanes] + jnp.where(first_half_sq, u0, u1)
            st_scr[s, p] = st
            for e in range(2):
                o_inter = inter[e][0] if ncb == 1 else jnp.concatenate(inter[e], axis=0)
                gate = gb_ref[s, rows, vlanes[e]]
                o = _rms(intra[e] + o_inter, og_ref[0]) * (gate * _sigmoid(gate))
                o_ref[s, rows, vlanes[e]] = o.astype(BF16)

    def block(j, carry):
        for s in range(ns):
            one_block(s, pl.multiple_of(j * r_len, r_len))
        return carry

    lax.fori_loop(0, nblk, block, 0)

    @pl.when(pl.program_id(1) == pl.num_programs(1) - 1)
    def _():
        for s in range(ns):
            for p in range(n_pairs):
                sout_ref[s, p * LANES:(p + 1) * LANES, :] = st_scr[s, p].T


def _gla(q, k, v, la, gb, og, li, s0, nb, t, c_len, *, ns, tb=512, r_len=256):
    tb = _row_tile(t, tb)
    r_len = _row_tile(tb, r_len)
    assert nb % ns == 0 and r_len % c_len == 0
    kblk = pl.BlockSpec((ns, tb, KB), lambda b, j: (b, j, 0))
    vblk = pl.BlockSpec((ns, tb, VB), lambda b, j: (b, j, 0))
    sblk = pl.BlockSpec((ns, KB, HEAD_V_B), lambda b, j: (b, 0, 0))
    per_stream = lambda a: a.reshape(nb, t, a.shape[-1])
    o, s_out = pl.pallas_call(
        functools.partial(_gla_kernel, ns=ns, nblk=tb // r_len, ncb=r_len // c_len, c_len=c_len),
        grid=(nb // ns, t // tb),
        in_specs=[kblk, kblk, vblk, kblk, vblk, _layer_spec(og, li), sblk],
        out_specs=[vblk, sblk],
        out_shape=[jax.ShapeDtypeStruct((nb, t, VB), BF16),
                   jax.ShapeDtypeStruct((nb, KB, HEAD_V_B), F32)],
        scratch_shapes=[pltpu.VMEM((ns, N_HEADS_B // 2, LANES, LANES), F32)],
        compiler_params=_params("parallel", "arbitrary"),
        name="gla",
    )(per_stream(q), per_stream(k), per_stream(v), per_stream(la), per_stream(gb), og, s0)
    return o.reshape(nb * t, VB), s_out


def _conv_kernel(x_ref, init_ref, g_ref, w1_ref, b1_ref, wdw_ref, bdw_ref, lng_ref, lnb_ref,
                 w2_ref, b2_ref, o_ref, tail_ref, buf_scr, *, ns, tt):
    d = x_ref.shape[1]
    lo = CONV_HALO - (CONV_W - 1)

    @pl.when(pl.program_id(1) == 0)
    def _():
        for s in range(ns):
            buf_scr[s, lo:CONV_HALO, :] = init_ref[s]
            buf_scr[s, CONV_HALO + tt:CONV_HALO + tt + SUBLANES, :] = jnp.zeros((SUBLANES, d), F32)

    x = x_ref[...]
    u = _mm(_rms(x, g_ref[0]).astype(BF16), w1_ref[0]) + b1_ref[0]
    glu = u[:, :d] * _sigmoid(u[:, d:])
    ys = []
    for s in range(ns):
        buf_scr[s, CONV_HALO:CONV_HALO + tt, :] = glu[s * tt:(s + 1) * tt]
        y = bdw_ref[0]
        for r in range(SUBLANES):
            part = None
            for k in range(lo, lo + CONV_W):
                if k % SUBLANES == r:
                    term = wdw_ref[0, k - lo:k - lo + 1, :] * buf_scr[s, k - r:k - r + tt + SUBLANES, :]
                    part = term if part is None else part + term
            y = y + part[r:r + tt, :]
        ys.append(y)
        tail = buf_scr[s, tt:tt + CONV_HALO, :]
        tail_ref[s] = tail
        buf_scr[s, 0:CONV_HALO, :] = tail
    y = ys[0] if ns == 1 else jnp.concatenate(ys, axis=0)
    mu = jnp.mean(y, axis=-1, keepdims=True)
    yc = y - mu
    var = jnp.mean(yc * yc, axis=-1, keepdims=True)
    yn = yc * lax.rsqrt(var + EPS) * lng_ref[0] + lnb_ref[0]
    z = (yn * _sigmoid(yn)).astype(BF16)
    o_ref[...] = x + _mm(z, w2_ref[0]) + b2_ref[0]


def _conv(x, init, g, l, conv_params, j, nb, t, *, tt=512, ns=8):
    rows, d = x.shape
    tt = _row_tile(t, tt)
    nt = t // tt
    ns = ns if nt == 1 else 1
    assert nb % ns == 0
    xblk = pl.BlockSpec((ns * tt, d), lambda b, i: (b * nt + i, 0))
    return pl.pallas_call(
        functools.partial(_conv_kernel, ns=ns, tt=tt),
        grid=(nb // ns, nt),
        in_specs=[xblk, pl.BlockSpec((ns, CONV_W - 1, d), lambda b, i: (b, 0, 0)), _layer_spec(g, l)]
                 + [_layer_spec(a, j) for a in conv_params],
        out_specs=[xblk, pl.BlockSpec((ns, CONV_HALO, d), lambda b, i: (b, 0, 0))],
        out_shape=[jax.ShapeDtypeStruct((rows, d), F32),
                   jax.ShapeDtypeStruct((nb, CONV_HALO, d), F32)],
        scratch_shapes=[pltpu.VMEM((ns, CONV_HALO + tt + SUBLANES, d), F32)],
        compiler_params=_params("parallel", "arbitrary"),
        name="conv",
    )(x, init, g, *conv_params)


def _band_bias(table):
    n_far = PAD + CHUNK - REL_MAX
    ext = jnp.concatenate([jnp.repeat(table[..., -1:], n_far, axis=-1), table[..., -2::-1]], axis=-1)
    period = ext.shape[-1] + 1
    ext = jnp.pad(ext, ((0, 0), (0, 0), (0, 1)))
    wrapped = jnp.tile(ext, (1, 1, CHUNK + 1))[..., :CHUNK * (period + 1)]
    wrapped = wrapped.reshape(ext.shape[:2] + (CHUNK, period + 1))
    return wrapped[:, :, ::-1, :BAND]


def _group_bias(bias):
    per_chunk = [jnp.pad(bias, ((0, 0), (0, 0), (c * CHUNK, (GROUP - 1 - c) * CHUNK)), constant_values=-1e30)
                 for c in range(GROUP)]
    stacked = jnp.stack(per_chunk, axis=1)
    return stacked.reshape(N_HEADS_A // 2, 2 * GROUP_ROWS, WINDOW)


def _positions_minor(a):
    n_l, n_b, n_t = a.shape[:3]
    return jnp.transpose(a, (0, 1, 3, 4, 2)).reshape(n_l, n_b, WA, n_t)


def _heads_minor(a):
    n_l, n_b, _, n_t = a.shape
    return jnp.transpose(a.reshape(n_l, n_b, N_HEADS_A, HEAD_DIM_A, n_t), (0, 1, 4, 2, 3))


def kernel(x_prompt, x_sample, cache_a_k, cache_a_v, state_b, cache_c, norm_mix_g, norm_ffn_g, w_in_ab, a_q_norm_g, a_k_norm_g, a_rel_bias, b_gate_w_up, b_gate_b, b_out_norm_g, w_out_ab, c_w_pw1, c_b_pw1, c_w_dw, c_b_dw, c_ln_g, c_ln_b, c_w_pw2, c_b_pw2, w_ffn_up, w_ffn_down):
    nbp, tp, d = x_prompt.shape
    nbs, ts, _ = x_sample.shape
    depth = norm_mix_g.shape[0]
    lp = min(PAD, tp)
    lc = cache_a_k.shape[2]
    assert lc == PAD and lp == PAD and ts <= CHUNK and tp % CHUNK == 0
    xp = x_prompt.reshape(nbp * tp, d)
    xs = x_sample.reshape(nbs * ts, d)
    vec = lambda a: a.reshape(a.shape[0], 1, a.shape[1])

    head_id = np.arange(MXU_TILE) // HEAD_DIM_A
    bd = jnp.asarray(head_id[:, None] == head_id[None, :], BF16)
    g_mix, g_ffn = vec(norm_mix_g), vec(norm_ffn_g)
    w_up, w_down = w_ffn_up.astype(BF16), w_ffn_down.astype(BF16)
    w_in = w_in_ab.astype(BF16)
    w_lr = jnp.pad(w_in_ab[:, :, P_MAIN:], ((0, 0), (0, 0), (0, LANES - GATE_RANK))).astype(BF16)
    gate_w = jnp.pad(b_gate_w_up, ((0, 0), (0, LANES - GATE_RANK), (0, 0))).astype(BF16)
    proj = (w_in, w_lr, gate_w, vec(b_gate_b),
            vec(jnp.tile(a_q_norm_g, (1, N_HEADS_A))), vec(jnp.tile(a_k_norm_g, (1, N_HEADS_A))))
    og = vec(b_out_norm_g)
    w_out = w_out_ab.astype(BF16)
    conv_params = (c_w_pw1.astype(BF16), vec(c_b_pw1), c_w_dw, vec(c_b_dw), vec(c_ln_g), vec(c_ln_b),
                   c_w_pw2.astype(BF16), vec(c_b_pw2))
    bias = _band_bias(a_rel_bias)
    cache_kt, cache_vt = _positions_minor(cache_a_k), _positions_minor(cache_a_v)
    pair_rows = lambda a: a.reshape(N_HEADS_A // 2, 2 * ts, a.shape[-1])

    akp, avp, aks, avs, sbp, sbs, ccp, ccs = [], [], [], [], [], [], [], []
    for l in range(depth):
        if l % 2 == 0:
            i = l // 2
            q, k, v, kt, vt, qb, kb, la, vb, gb = _inproj(xp, g_mix, l, *proj, i, bd, stream_len=tp)
            oa = _attn_prompt(q, k, v, _group_bias(bias[i]), nbp, tp)
            ob, sb = _gla(qb, kb, vb, la, gb, og, i, jnp.zeros((nbp, KB, HEAD_V_B), F32), nbp, tp, CHUNK,
                          ns=min(nbp, 2))
            akp.append(kt)
            avp.append(vt)
            sbp.append(sb.reshape(nbp, N_HEADS_B, HEAD_K_B, HEAD_V_B))
            merge_p = (oa, ob, w_out, i)

            q, k, v, qb, kb, la, vb, gb = _inproj(xs, g_mix, l, *proj, i, bd)
            oa = _attn_sample(q, k, v, cache_kt, cache_vt, i,
                              pair_rows(bias[i, :, :ts, :lc]), pair_rows(bias[i, :, :ts, lc:lc + ts]), nbs, ts,
                              ns=min(nbs, 4))
            ob, sb = _gla(qb, kb, vb, la, gb, og, i, state_b[i].reshape(nbs, KB, HEAD_V_B), nbs, ts, ts,
                          ns=min(nbs, 8))
            aks.append(k.reshape(nbs, ts, N_HEADS_A, HEAD_DIM_A))
            avs.append(v.reshape(nbs, ts, N_HEADS_A, HEAD_DIM_A))
            sbs.append(sb.reshape(nbs, N_HEADS_B, HEAD_K_B, HEAD_V_B))
            merge_s = (oa, ob, w_out, i)
        else:
            merge_p = merge_s = None
            j = l // 2
            xp, tail = _conv(xp, jnp.zeros((nbp, CONV_W - 1, d), F32), g_mix, l, conv_params, j, nbp, tp)
            ccp.append(tail[:, CONV_HALO - (CONV_W - 1):])
            xs, tail = _conv(xs, cache_c[j], g_mix, l, conv_params, j, nbs, ts, ns=min(nbs, 8))
            ccs.append(tail[:, CONV_HALO - (CONV_W - 1):])
        xp = _ffn(xp, g_ffn, w_up, w_down, l, merge_p)
        xs = _ffn(xs, g_ffn, w_up, w_down, l, merge_s)
    return (xp.reshape(nbp, tp, d), xs.reshape(nbs, ts, d),
            _heads_minor(jnp.stack(akp)), _heads_minor(jnp.stack(avp)), jnp.stack(aks), jnp.stack(avs),
            jnp.stack(sbp), jnp.stack(sbs), jnp.stack(ccp), jnp.stack(ccs))
```

```python
import functools

import numpy as np
import jax
import jax.numpy as jnp
from jax import lax
from jax.experimental import pallas as pl
from jax.experimental.pallas import tpu as pltpu

F32 = jnp.float32
BF16 = jnp.bfloat16

EPS = 1e-6
CHUNK = 64
PAST_CHUNKS = 8
PAD = PAST_CHUNKS * CHUNK
BAND = PAD + CHUNK
N_HEADS_A = 8
HEAD_DIM_A = 64
REL_MIN = -(CHUNK - 1)
REL_MAX = 2 * CHUNK
N_HEADS_B = 4
HEAD_K_B = 64
HEAD_V_B = 128
GATE_RANK = 16
GATE_NORM = 16.0
CONV_W = 31
CONV_HALO = 32
WA = N_HEADS_A * HEAD_DIM_A
KB = N_HEADS_B * HEAD_K_B
VB = N_HEADS_B * HEAD_V_B
P_MAIN = 3 * WA + 2 * KB + 2 * VB
LANES = 128
SUBLANES = 8
MXU_TILE = 256
VMEM_LIMIT = 56 * 1024 * 1024


def _mm(a, b):
    return jnp.dot(a, b, preferred_element_type=F32)


def _mm_nt(a, b):
    return lax.dot_general(a, b, (((1,), (1,)), ((), ())), preferred_element_type=F32)


def _mm_tn(a, b):
    return lax.dot_general(a, b, (((0,), (0,)), ((), ())), preferred_element_type=F32)


def _rms(x, g):
    ms = jnp.mean(x * x, axis=-1, keepdims=True)
    return x * lax.rsqrt(ms + EPS) * g


def _sigmoid(x):
    return 1.0 / (1.0 + jnp.exp(-x))


def _params(*sem):
    return pltpu.CompilerParams(dimension_semantics=sem, vmem_limit_bytes=VMEM_LIMIT)


def _row_tile(rows, want):
    tm = min(rows, want)
    assert rows % tm == 0, (rows, tm)
    return tm


def _layer_spec(a, l):
    return pl.BlockSpec((1,) + a.shape[1:], lambda *_: (l,) + (0,) * (a.ndim - 1))


def _full_spec(a):
    return pl.BlockSpec(a.shape, lambda *_: (0,) * a.ndim)


def _ffn_steps(x1, g_ref, wu_ref, wd_ref, o_ref, xn_scr, n_sub):
    @pl.when(pl.program_id(1) == 0)
    def _():
        x = x1()
        o_ref[...] = x
        xn_scr[...] = _rms(x, g_ref[0]).astype(BF16)

    sub = wu_ref.shape[2] // n_sub
    for c in range(n_sub):
        h = jnp.maximum(_mm(xn_scr[...], wu_ref[0, :, c * sub:(c + 1) * sub]), 0.0)
        o_ref[...] += _mm((h * h).astype(BF16), wd_ref[0, c * sub:(c + 1) * sub, :])


def _ffn_kernel(x_ref, g_ref, wu_ref, wd_ref, o_ref, xn_scr, *, n_sub):
    _ffn_steps(lambda: x_ref[...], g_ref, wu_ref, wd_ref, o_ref, xn_scr, n_sub)


def _merge_ffn_kernel(x_ref, oa_ref, ob_ref, wa_ref, wb_ref, g_ref, wu_ref, wd_ref, o_ref, xn_scr,
                      *, n_sub):
    def merged():
        return x_ref[...] + _mm(oa_ref[...], wa_ref[0]) + _mm(ob_ref[...], wb_ref[0])

    _ffn_steps(merged, g_ref, wu_ref, wd_ref, o_ref, xn_scr, n_sub)


def _ffn(x, g, w_up, w_down, l, merge=None, *, tm=1024, tf=2048):
    rows, d = x.shape
    dff = w_up.shape[2]
    tm = _row_tile(rows, tm)
    n_sub = dff // _row_tile(dff, tf if merge is None else tf // 2)
    tf = dff
    once = pl.Buffered(1)
    row = lambda w: pl.BlockSpec((tm, w), lambda i, j: (i, 0))
    ffn_specs = [_layer_spec(g, l), pl.BlockSpec((1, d, tf), lambda i, j: (l, 0, j), pipeline_mode=once),
                 pl.BlockSpec((1, tf, d), lambda i, j: (l, j, 0), pipeline_mode=once)]
    if merge is None:
        body, args, specs = _ffn_kernel, (x,), [row(d)]
    else:
        oa, ob, w_out, li = merge
        body, args = _merge_ffn_kernel, (x, oa, ob, w_out, w_out)
        specs = [row(d), row(WA), row(VB),
                 pl.BlockSpec((1, WA, d), lambda i, j: (li, 0, 0)),
                 pl.BlockSpec((1, VB, d), lambda i, j: (li, WA // VB, 0))]
    return pl.pallas_call(
        functools.partial(body, n_sub=n_sub),
        grid=(rows // tm, dff // tf),
        in_specs=specs + ffn_specs,
        out_specs=row(d),
        out_shape=jax.ShapeDtypeStruct((rows, d), F32),
        scratch_shapes=[pltpu.VMEM((tm, d), BF16)],
        compiler_params=_params("parallel", "arbitrary"),
        name="ffn" if merge is None else "merge_ffn",
    )(*args, g, w_up, w_down)


def _inproj_kernel(x_ref, g_ref, w_ref, wlr_ref, gw_ref, gbias_ref, qg_ref, kg_ref, bd_ref, *outs,
                   tail_every):
    if tail_every:
        q_o, k_o, v_o, kt_o, vt_o, qb_o, kb_o, la_o, vb_o, gb_o = outs
    else:
        q_o, k_o, v_o, qb_o, kb_o, la_o, vb_o, gb_o = outs
    xn = _rms(x_ref[...], g_ref[0]).astype(BF16)

    def seg(a, b):
        return _mm(xn, w_ref[0, :, a:b])

    def head_norm(h, gain):
        sq = (h * h).astype(BF16)
        w = bd_ref.shape[0]
        ss = jnp.concatenate([_mm(sq[:, c:c + w], bd_ref[...]) for c in range(0, WA, w)], axis=1)
        return h * lax.rsqrt(ss * (1.0 / HEAD_DIM_A) + EPS) * gain

    q_o[...] = (head_norm(seg(0, WA), qg_ref[0]) * (HEAD_DIM_A ** -0.5)).astype(BF16)
    k = head_norm(seg(WA, 2 * WA), kg_ref[0])
    v = seg(2 * WA, 3 * WA)
    k_o[...] = k.astype(k_o.dtype)
    v_o[...] = v.astype(v_o.dtype)
    if tail_every:
        @pl.when(pl.program_id(0) % tail_every == tail_every - 1)
        def _():
            kt_o[0] = k.T
            vt_o[0] = v.T
    o = 3 * WA
    hb = seg(o, o + 2 * KB)
    qb_o[...] = hb[:, :KB] * (HEAD_K_B ** -0.5)
    kb_o[...] = hb[:, KB:]
    vb_o[...] = seg(o + 2 * KB, o + 2 * KB + VB).astype(BF16)
    gb_o[...] = seg(o + 2 * KB + VB, P_MAIN)
    lr = _mm(xn, wlr_ref[0])
    logit = _mm(lr.astype(BF16), gw_ref[0]) + gbias_ref[0]
    log_sig = jnp.minimum(logit, 0.0) - jnp.log1p(jnp.exp(-jnp.abs(logit)))
    la_o[...] = log_sig * (1.0 / GATE_NORM)


def _inproj(x, g, l, w_in, w_lr, gate_w, gate_b, qg, kg, i, bd, *, stream_len=None, tm=512):
    rows, d = x.shape
    tm = _row_tile(rows, tm)
    row = lambda w: pl.BlockSpec((tm, w), lambda r: (r, 0))
    kv_dtype = BF16 if stream_len else F32
    outs = [(WA, BF16), (WA, kv_dtype), (WA, kv_dtype), (KB, F32), (KB, F32), (KB, F32), (VB, BF16), (VB, F32)]
    out_specs = [row(w) for w, _ in outs]
    out_shape = [jax.ShapeDtypeStruct((rows, w), dt) for w, dt in outs]
    tail_every = None
    if stream_len:
        assert stream_len % tm == 0 and tm == PAD
        tail_every = stream_len // tm
        tail_spec = pl.BlockSpec((1, WA, tm), lambda r: (r // tail_every, 0, 0))
        tail_shape = jax.ShapeDtypeStruct((rows // stream_len, WA, tm), F32)
        out_specs[3:3] = [tail_spec, tail_spec]
        out_shape[3:3] = [tail_shape, tail_shape]
    return pl.pallas_call(
        functools.partial(_inproj_kernel, tail_every=tail_every),
        grid=(rows // tm,),
        in_specs=[row(d), _layer_spec(g, l), _layer_spec(w_in, i), _layer_spec(w_lr, i),
                  _layer_spec(gate_w, i), _layer_spec(gate_b, i), _layer_spec(qg, i),
                  _layer_spec(kg, i), _full_spec(bd)],
        out_specs=out_specs,
        out_shape=out_shape,
        compiler_params=_params("arbitrary"),
        name="inproj",
    )(x, g, w_in, w_lr, gate_w, gate_b, qg, kg, bd)


def _softmax_pv(scores, pv_fns):
    m = scores[0].max(axis=-1, keepdims=True)
    for s in scores[1:]:
        m = jnp.maximum(m, s.max(axis=-1, keepdims=True))
    l = 0.0
    pv = 0.0
    for s, fn in zip(scores, pv_fns):
        p = jnp.exp(s - m)
        l = l + p.sum(axis=-1, keepdims=True)
        pv = pv + fn(p.astype(BF16))
    return pv / l


def _stack_pair(q2, first_half):
    zero = jnp.zeros_like(q2)
    return jnp.concatenate([jnp.where(first_half, q2, zero), jnp.where(first_half, zero, q2)], axis=0)


GROUP = 4
GROUP_ROWS = GROUP * CHUNK
WINDOW = PAD + GROUP_ROWS


def _attn_prompt_kernel(q_ref, k_ref, v_ref, bias_ref, o_ref, *, ng):
    first_half = lax.broadcasted_iota(jnp.int32, (GROUP_ROWS, LANES), 1) < HEAD_DIM_A

    def group(r0, keys, bias_cols):
        for p in range(N_HEADS_A // 2):
            lanes = slice(p * LANES, (p + 1) * LANES)
            kw = k_ref[keys, lanes]
            vw = v_ref[keys, lanes]
            qs = _stack_pair(q_ref[pl.ds(r0, GROUP_ROWS), lanes], first_half)
            s = _mm_nt(qs, kw) + bias_ref[p, :, bias_cols]
            o = _softmax_pv([s], [lambda pb: _mm(pb, vw)])
            o_ref[pl.ds(r0, GROUP_ROWS), lanes] = jnp.where(
                first_half, o[:GROUP_ROWS], o[GROUP_ROWS:]).astype(BF16)

    n_ramp = min(ng, PAD // GROUP_ROWS)
    for g in range(n_ramp):
        n_keys = (g + 1) * GROUP_ROWS
        group(g * GROUP_ROWS, slice(0, n_keys), slice(WINDOW - n_keys, WINDOW))

    def full_group(g, carry):
        r0 = pl.multiple_of(g * GROUP_ROWS, GROUP_ROWS)
        group(r0, pl.ds(pl.multiple_of(r0 - PAD, GROUP_ROWS), WINDOW), slice(0, WINDOW))
        return carry

    lax.fori_loop(n_ramp, ng, full_group, 0, unroll=2)


def _attn_prompt(q, k, v, bias, nb, t):
    ng = t // GROUP_ROWS
    blk = pl.BlockSpec((t, WA), lambda b: (b, 0))
    return pl.pallas_call(
        functools.partial(_attn_prompt_kernel, ng=ng),
        grid=(nb,),
        in_specs=[blk, blk, blk, _full_spec(bias)],
        out_specs=blk,
        out_shape=jax.ShapeDtypeStruct((nb * t, WA), BF16),
        compiler_params=_params("parallel"),
        name="attn_prompt",
    )(q, k, v, bias)


def _attn_sample_kernel(q_ref, kn_ref, vn_ref, ckt_ref, cvt_ref, bias_c_ref, bias_n_ref, o_ref, *, ns, t):
    first_half = lax.broadcasted_iota(jnp.int32, (t, LANES), 1) < HEAD_DIM_A
    for s in range(ns):
        rows = slice(s * t, (s + 1) * t)
        for p in range(N_HEADS_A // 2):
            lanes = slice(p * LANES, (p + 1) * LANES)
            qs = _stack_pair(q_ref[rows, lanes], first_half)
            kct = ckt_ref[0, s, lanes, :].astype(BF16)
            vct = cvt_ref[0, s, lanes, :].astype(BF16)
            kn = kn_ref[rows, lanes].astype(BF16)
            vn = vn_ref[rows, lanes].astype(BF16)
            s_c = _mm(qs, kct) + bias_c_ref[p]
            s_n = _mm_nt(qs, kn) + bias_n_ref[p]
            o = _softmax_pv([s_c, s_n], [lambda pb: _mm_nt(pb, vct), lambda pb: _mm(pb, vn)])
            o_ref[rows, lanes] = jnp.where(first_half, o[:t], o[t:]).astype(BF16)


def _attn_sample(q, k_new, v_new, cache_kt, cache_vt, li, bias_c, bias_n, nb, t, *, ns=4):
    lc = cache_kt.shape[3]
    assert nb % ns == 0
    blk = pl.BlockSpec((ns * t, WA), lambda b: (b, 0))
    cblk = pl.BlockSpec((1, ns, WA, lc), lambda b: (li, b, 0, 0))
    return pl.pallas_call(
        functools.partial(_attn_sample_kernel, ns=ns, t=t),
        grid=(nb // ns,),
        in_specs=[blk, blk, blk, cblk, cblk, _full_spec(bias_c), _full_spec(bias_n)],
        out_specs=blk,
        out_shape=jax.ShapeDtypeStruct((nb * t, WA), BF16),
        compiler_params=_params("parallel"),
        name="attn_sample",
    )(q, k_new, v_new, cache_kt, cache_vt, bias_c, bias_n)


def _gla_kernel(q_ref, k_ref, v_ref, la_ref, gb_ref, og_ref, s0_ref, o_ref, sout_ref, st_scr,
                *, ns, nblk, ncb, c_len):
    n_pairs = N_HEADS_B // 2
    r_len = ncb * c_len

    @pl.when(pl.program_id(1) == 0)
    def _():
        for s in range(ns):
            for p in range(n_pairs):
                st_scr[s, p] = s0_ref[s, p * LANES:(p + 1) * LANES, :].T

    ri = lax.broadcasted_iota(jnp.int32, (r_len, r_len), 0)
    ci = lax.broadcasted_iota(jnp.int32, (r_len, r_len), 1)
    same_chunk = (ri // c_len) == (ci // c_len)
    causal = same_chunk & (ri >= ci)
    causal2 = jnp.concatenate([causal, causal], axis=0)
    tril = causal.astype(BF16)
    ones_bd = same_chunk.astype(BF16)
    row_chunk = lax.broadcasted_iota(jnp.int32, (r_len, HEAD_V_B), 0) // c_len
    first_half = lax.broadcasted_iota(jnp.int32, (r_len, LANES), 1) < HEAD_K_B
    first_half_sq = lax.broadcasted_iota(jnp.int32, (LANES, LANES), 1) < HEAD_K_B

    def one_block(s, r0):
        rows = pl.ds(r0, r_len)
        la = la_ref[s, rows, :]
        la_hi = la.astype(BF16)
        la_lo = (la - la_hi.astype(F32)).astype(BF16)
        b = _mm(tril, la_hi) + _mm(tril, la_lo)
        bl = _mm(ones_bd, la_hi) + _mm(ones_bd, la_lo)
        ref = 0.5 * bl
        q = q_ref[s, rows, :]
        k = k_ref[s, rows, :]
        qt = (q * jnp.exp(b - ref)).astype(BF16)
        kt = (k * jnp.exp(ref - b)).astype(BF16)
        qd = (q * jnp.exp(b)).astype(BF16)
        kd = (k * jnp.exp(bl - b)).astype(BF16)
        dl = jnp.exp(bl)
        for p in range(n_pairs):
            lanes = slice(p * LANES, (p + 1) * LANES)
            kt2 = kt[:, lanes]
            kd2 = kd[:, lanes]
            vlanes = [slice((2 * p + e) * HEAD_V_B, (2 * p + e + 1) * HEAD_V_B) for e in range(2)]
            vs = [v_ref[s, rows, vl] for vl in vlanes]
            a = jnp.where(causal2, _mm_nt(_stack_pair(qt[:, lanes], first_half), kt2), 0.0).astype(BF16)
            intra = [_mm(a[e * r_len:(e + 1) * r_len], vs[e]) for e in range(2)]
            qd_stack = _stack_pair(qd[:, lanes], first_half)
            v_bd = jnp.concatenate([jnp.where(row_chunk == c, vs[e], jnp.zeros_like(vs[e]))
                                    for e in range(2) for c in range(ncb)], axis=1)
            upd = _mm_tn(v_bd, kd2)
            st = st_scr[s, p]
            inter = [[], []]
            for c in range(ncb):
                q_c = jnp.concatenate([qd_stack[e * r_len + c * c_len:e * r_len + (c + 1) * c_len] for e in range(2)], axis=0)
                o_c = _mm_nt(q_c, st.astype(BF16))
                inter[0].append(o_c[:c_len])
                inter[1].append(o_c[c_len:])
                u0 = upd[c * HEAD_V_B:(c + 1) * HEAD_V_B]
                u1 = upd[(ncb + c) * HEAD_V_B:(ncb + c + 1) * HEAD_V_B]
                st = st * dl[c * c_len:c * c_len + 1, lanes] + jnp.where(first_half_sq, u0, u1)
            st_scr[s, p] = st
            for e in range(2):
                o_inter = inter[e][0] if ncb == 1 else jnp.concatenate(inter[e], axis=0)
                gate = gb_ref[s, rows, vlanes[e]]
                o = _rms(intra[e] + o_inter, og_ref[0]) * (gate * _sigmoid(gate))
                o_ref[s, rows, vlanes[e]] = o.astype(BF16)

    def block(j, carry):
        for s in range(ns):
            one_block(s, pl.multiple_of(j * r_len, r_len))
        return carry

    lax.fori_loop(0, nblk, block, 0)

    @pl.when(pl.program_id(1) == pl.num_programs(1) - 1)
    def _():
        for s in range(ns):
            for p in range(n_pairs):
                sout_ref[s, p * LANES:(p + 1) * LANES, :] = st_scr[s, p].T


def _gla(q, k, v, la, gb, og, li, s0, nb, t, c_len, *, ns, tb=512, r_len=256):
    tb = _row_tile(t, tb)
    r_len = _row_tile(tb, r_len)
    assert nb % ns == 0 and r_len % c_len == 0
    kblk = pl.BlockSpec((ns, tb, KB), lambda b, j: (b, j, 0))
    vblk = pl.BlockSpec((ns, tb, VB), lambda b, j: (b, j, 0))
    sblk = pl.BlockSpec((ns, KB, HEAD_V_B), lambda b, j: (b, 0, 0))
    per_stream = lambda a: a.reshape(nb, t, a.shape[-1])
    o, s_out = pl.pallas_call(
        functools.partial(_gla_kernel, ns=ns, nblk=tb // r_len, ncb=r_len // c_len, c_len=c_len),
        grid=(nb // ns, t // tb),
        in_specs=[kblk, kblk, vblk, kblk, vblk, _layer_spec(og, li), sblk],
        out_specs=[vblk, sblk],
        out_shape=[jax.ShapeDtypeStruct((nb, t, VB), BF16),
                   jax.ShapeDtypeStruct((nb, KB, HEAD_V_B), F32)],
        scratch_shapes=[pltpu.VMEM((ns, N_HEADS_B // 2, LANES, LANES), F32)],
        compiler_params=_params("parallel", "arbitrary"),
        name="gla",
    )(per_stream(q), per_stream(k), per_stream(v), per_stream(la), per_stream(gb), og, s0)
    return o.reshape(nb * t, VB), s_out


def _conv_kernel(x_ref, init_ref, g_ref, w1_ref, b1_ref, wdw_ref, bdw_ref, lng_ref, lnb_ref,
                 w2_ref, b2_ref, o_ref, tail_ref, buf_scr, *, ns, tt):
    d = x_ref.shape[1]
    lo = CONV_HALO - (CONV_W - 1)

    @pl.when(pl.program_id(1) == 0)
    def _():
        for s in range(ns):
            buf_scr[s, lo:CONV_HALO, :] = init_ref[s]
            buf_scr[s, CONV_HALO + tt:CONV_HALO + tt + SUBLANES, :] = jnp.zeros((SUBLANES, d), F32)

    x = x_ref[...]
    u = _mm(_rms(x, g_ref[0]).astype(BF16), w1_ref[0]) + b1_ref[0]
    glu = u[:, :d] * _sigmoid(u[:, d:])
    ys = []
    for s in range(ns):
        buf_scr[s, CONV_HALO:CONV_HALO + tt, :] = glu[s * tt:(s + 1) * tt]
        y = bdw_ref[0]
        for r in range(SUBLANES):
            part = None
            for k in range(lo, lo + CONV_W):
                if k % SUBLANES == r:
                    term = wdw_ref[0, k - lo:k - lo + 1, :] * buf_scr[s, k - r:k - r + tt + SUBLANES, :]
                    part = term if part is None else part + term
            y = y + part[r:r + tt, :]
        ys.append(y)
        tail = buf_scr[s, tt:tt + CONV_HALO, :]
        tail_ref[s] = tail
        buf_scr[s, 0:CONV_HALO, :] = tail
    y = ys[0] if ns == 1 else jnp.concatenate(ys, axis=0)
    mu = jnp.mean(y, axis=-1, keepdims=True)
    yc = y - mu
    var = jnp.mean(yc * yc, axis=-1, keepdims=True)
    yn = yc * lax.rsqrt(var + EPS) * lng_ref[0] + lnb_ref[0]
    z = (yn * _sigmoid(yn)).astype(BF16)
    o_ref[...] = x + _mm(z, w2_ref[0]) + b2_ref[0]


def _conv(x, init, g, l, conv_params, j, nb, t, *, tt=512, ns=8):
    rows, d = x.shape
    tt = _row_tile(t, tt)
    nt = t // tt
    ns = ns if nt == 1 else 1
    assert nb % ns == 0
    xblk = pl.BlockSpec((ns * tt, d), lambda b, i: (b * nt + i, 0))
    return pl.pallas_call(
        functools.partial(_conv_kernel, ns=ns, tt=tt),
        grid=(nb // ns, nt),
        in_specs=[xblk, pl.BlockSpec((ns, CONV_W - 1, d), lambda b, i: (b, 0, 0)), _layer_spec(g, l)]
                 + [_layer_spec(a, j) for a in conv_params],
        out_specs=[xblk, pl.BlockSpec((ns, CONV_HALO, d), lambda b, i: (b, 0, 0))],
        out_shape=[jax.ShapeDtypeStruct((rows, d), F32),
                   jax.ShapeDtypeStruct((nb, CONV_HALO, d), F32)],
        scratch_shapes=[pltpu.VMEM((ns, CONV_HALO + tt + SUBLANES, d), F32)],
        compiler_params=_params("parallel", "arbitrary"),
        name="conv",
    )(x, init, g, *conv_params)


def _band_bias(table):
    n_far = PAD + CHUNK - REL_MAX
    ext = jnp.concatenate([jnp.repeat(table[..., -1:], n_far, axis=-1), table[..., -2::-1]], axis=-1)
    period = ext.shape[-1] + 1
    ext = jnp.pad(ext, ((0, 0), (0, 0), (0, 1)))
    wrapped = jnp.tile(ext, (1, 1, CHUNK + 1))[..., :CHUNK * (period + 1)]
    wrapped = wrapped.reshape(ext.shape[:2] + (CHUNK, period + 1))
    return wrapped[:, :, ::-1, :BAND]


def _group_bias(bias):
    per_chunk = [jnp.pad(bias, ((0, 0), (0, 0), (c * CHUNK, (GROUP - 1 - c) * CHUNK)), constant_values=-1e30)
                 for c in range(GROUP)]
    stacked = jnp.stack(per_chunk, axis=1)
    return stacked.reshape(N_HEADS_A // 2, 2 * GROUP_ROWS, WINDOW)


def _positions_minor(a):
    n_l, n_b, n_t = a.shape[:3]
    return jnp.transpose(a, (0, 1, 3, 4, 2)).reshape(n_l, n_b, WA, n_t)


def _heads_minor(a):
    n_l, n_b, _, n_t = a.shape
    return jnp.transpose(a.reshape(n_l, n_b, N_HEADS_A, HEAD_DIM_A, n_t), (0, 1, 4, 2, 3))


def kernel(x_prompt, x_sample, cache_a_k, cache_a_v, state_b, cache_c, norm_mix_g, norm_ffn_g, w_in_ab, a_q_norm_g, a_k_norm_g, a_rel_bias, b_gate_w_up, b_gate_b, b_out_norm_g, w_out_ab, c_w_pw1, c_b_pw1, c_w_dw, c_b_dw, c_ln_g, c_ln_b, c_w_pw2, c_b_pw2, w_ffn_up, w_ffn_down):
    nbp, tp, d = x_prompt.shape
    nbs, ts, _ = x_sample.shape
    depth = norm_mix_g.shape[0]
    lp = min(PAD, tp)
    lc = cache_a_k.shape[2]
    assert lc == PAD and lp == PAD and ts <= CHUNK and tp % CHUNK == 0
    xp = x_prompt.reshape(nbp * tp, d)
    xs = x_sample.reshape(nbs * ts, d)
    vec = lambda a: a.reshape(a.shape[0], 1, a.shape[1])

    head_id = np.arange(MXU_TILE) // HEAD_DIM_A
    bd = jnp.asarray(head_id[:, None] == head_id[None, :], BF16)
    g_mix, g_ffn = vec(norm_mix_g), vec(norm_ffn_g)
    w_up, w_down = w_ffn_up.astype(BF16), w_ffn_down.astype(BF16)
    w_in = w_in_ab.astype(BF16)
    w_lr = jnp.pad(w_in_ab[:, :, P_MAIN:], ((0, 0), (0, 0), (0, LANES - GATE_RANK))).astype(BF16)
    gate_w = jnp.pad(b_gate_w_up, ((0, 0), (0, LANES - GATE_RANK), (0, 0))).astype(BF16)
    proj = (w_in, w_lr, gate_w, vec(b_gate_b),
            vec(jnp.tile(a_q_norm_g, (1, N_HEADS_A))), vec(jnp.tile(a_k_norm_g, (1, N_HEADS_A))))
    og = vec(b_out_norm_g)
    w_out = w_out_ab.astype(BF16)
    conv_params = (c_w_pw1.astype(BF16), vec(c_b_pw1), c_w_dw, vec(c_b_dw), vec(c_ln_g), vec(c_ln_b),
                   c_w_pw2.astype(BF16), vec(c_b_pw2))
    bias = _band_bias(a_rel_bias)
    cache_kt, cache_vt = _positions_minor(cache_a_k), _positions_minor(cache_a_v)
    pair_rows = lambda a: a.reshape(N_HEADS_A // 2, 2 * ts, a.shape[-1])

    akp, avp, aks, avs, sbp, sbs, ccp, ccs = [], [], [], [], [], [], [], []
    for l in range(depth):
        if l % 2 == 0:
            i = l // 2
            q, k, v, kt, vt, qb, kb, la, vb, gb = _inproj(xp, g_mix, l, *proj, i, bd, stream_len=tp)
            oa = _attn_prompt(q, k, v, _group_bias(bias[i]), nbp, tp)
            ob, sb = _gla(qb, kb, vb, la, gb, og, i, jnp.zeros((nbp, KB, HEAD_V_B), F32), nbp, tp, CHUNK,
                          ns=min(nbp, 2))
            akp.append(kt)
            avp.append(vt)
            sbp.append(sb.reshape(nbp, N_HEADS_B, HEAD_K_B, HEAD_V_B))
            merge_p = (oa, ob, w_out, i)

            q, k, v, qb, kb, la, vb, gb = _inproj(xs, g_mix, l, *proj, i, bd)
            oa = _attn_sample(q, k, v, cache_kt, cache_vt, i,
                              pair_rows(bias[i, :, :ts, :lc]), pair_rows(bias[i, :, :ts, lc:lc + ts]), nbs, ts,
                              ns=min(nbs, 4))
            ob, sb = _gla(qb, kb, vb, la, gb, og, i, state_b[i].reshape(nbs, KB, HEAD_V_B), nbs, ts, ts,
                          ns=min(nbs, 8))
            aks.append(k.reshape(nbs, ts, N_HEADS_A, HEAD_DIM_A))
            avs.append(v.reshape(nbs, ts, N_HEADS_A, HEAD_DIM_A))
            sbs.append(sb.reshape(nbs, N_HEADS_B, HEAD_K_B, HEAD_V_B))
            merge_s = (oa, ob, w_out, i)
        else:
            merge_p = merge_s = None
            j = l // 2
            xp, tail = _conv(xp, jnp.zeros((nbp, CONV_W - 1, d), F32), g_mix, l, conv_params, j, nbp, tp)
            ccp.append(tail[:, CONV_HALO - (CONV_W - 1):])
            xs, tail = _conv(xs, cache_c[j], g_mix, l, conv_params, j, nbs, ts, ns=min(nbs, 8))
            ccs.append(tail[:, CONV_HALO - (CONV_W - 1):])
        xp = _ffn(xp, g_ffn, w_up, w_down, l, merge_p)
        xs = _ffn(xs, g_ffn, w_up, w_down, l, merge_s)
    return (xp.reshape(nbp, tp, d), xs.reshape(nbs, ts, d),
            _heads_minor(jnp.stack(akp)), _heads_minor(jnp.stack(avp)), jnp.stack(aks), jnp.stack(avs),
            jnp.stack(sbp), jnp.stack(sbs), jnp.stack(ccp), jnp.stack(ccs))
```

```python
import functools

import numpy as np
import jax
import jax.numpy as jnp
from jax import lax
from jax.experimental import pallas as pl
from jax.experimental.pallas import tpu as pltpu

F32 = jnp.float32
BF16 = jnp.bfloat16

EPS = 1e-6
CHUNK = 64
PAST_CHUNKS = 8
PAD = PAST_CHUNKS * CHUNK
BAND = PAD + CHUNK
N_HEADS_A = 8
HEAD_DIM_A = 64
REL_MIN = -(CHUNK - 1)
REL_MAX = 2 * CHUNK
N_HEADS_B = 4
HEAD_K_B = 64
HEAD_V_B = 128
GATE_RANK = 16
GATE_NORM = 16.0
CONV_W = 31
CONV_HALO = 32
WA = N_HEADS_A * HEAD_DIM_A
KB = N_HEADS_B * HEAD_K_B
VB = N_HEADS_B * HEAD_V_B
P_MAIN = 3 * WA + 2 * KB + 2 * VB
LANES = 128
SUBLANES = 8
MXU_TILE = 256
VMEM_LIMIT = 56 * 1024 * 1024


def _mm(a, b):
    return jnp.dot(a, b, preferred_element_type=F32)


def _mm_nt(a, b):
    return lax.dot_general(a, b, (((1,), (1,)), ((), ())), preferred_element_type=F32)


def _mm_tn(a, b):
    return lax.dot_general(a, b, (((0,), (0,)), ((), ())), preferred_element_type=F32)


def _rms(x, g):
    ms = jnp.mean(x * x, axis=-1, keepdims=True)
    return x * lax.rsqrt(ms + EPS) * g


def _sigmoid(x):
    return 1.0 / (1.0 + jnp.exp(-x))


def _params(*sem):
    return pltpu.CompilerParams(dimension_semantics=sem, vmem_limit_bytes=VMEM_LIMIT)


def _row_tile(rows, want):
    tm = min(rows, want)
    assert rows % tm == 0, (rows, tm)
    return tm


def _layer_spec(a, l):
    return pl.BlockSpec((1,) + a.shape[1:], lambda *_: (l,) + (0,) * (a.ndim - 1))


def _full_spec(a):
    return pl.BlockSpec(a.shape, lambda *_: (0,) * a.ndim)


def _ffn_steps(x1, g_ref, wu_ref, wd_ref, o_ref, xn_scr, n_sub):
    @pl.when(pl.program_id(1) == 0)
    def _():
        x = x1()
        o_ref[...] = x
        xn_scr[...] = _rms(x, g_ref[0]).astype(BF16)

    sub = wu_ref.shape[2] // n_sub
    for c in range(n_sub):
        h = jnp.maximum(_mm(xn_scr[...], wu_ref[0, :, c * sub:(c + 1) * sub]), 0.0)
        o_ref[...] += _mm((h * h).astype(BF16), wd_ref[0, c * sub:(c + 1) * sub, :])


def _ffn_kernel(x_ref, g_ref, wu_ref, wd_ref, o_ref, xn_scr, *, n_sub):
    _ffn_steps(lambda: x_ref[...], g_ref, wu_ref, wd_ref, o_ref, xn_scr, n_sub)


def _merge_ffn_kernel(x_ref, oa_ref, ob_ref, wa_ref, wb_ref, g_ref, wu_ref, wd_ref, o_ref, xn_scr,
                      *, n_sub):
    def merged():
        return x_ref[...] + _mm(oa_ref[...], wa_ref[0]) + _mm(ob_ref[...], wb_ref[0])

    _ffn_steps(merged, g_ref, wu_ref, wd_ref, o_ref, xn_scr, n_sub)


def _ffn(x, g, w_up, w_down, l, merge=None, *, tm=1024, tf=2048):
    rows, d = x.shape
    dff = w_up.shape[2]
    tm = _row_tile(rows, tm)
    n_sub = dff // _row_tile(dff, tf if merge is None else tf // 2)
    tf = dff
    once = pl.Buffered(1)
    row = lambda w: pl.BlockSpec((tm, w), lambda i, j: (i, 0))
    ffn_specs = [_layer_spec(g, l), pl.BlockSpec((1, d, tf), lambda i, j: (l, 0, j), pipeline_mode=once),
                 pl.BlockSpec((1, tf, d), lambda i, j: (l, j, 0), pipeline_mode=once)]
    if merge is None:
        body, args, specs = _ffn_kernel, (x,), [row(d)]
    else:
        oa, ob, w_out, li = merge
        body, args = _merge_ffn_kernel, (x, oa, ob, w_out, w_out)
        specs = [row(d), row(WA), row(VB),
                 pl.BlockSpec((1, WA, d), lambda i, j: (li, 0, 0)),
                 pl.BlockSpec((1, VB, d), lambda i, j: (li, WA // VB, 0))]
    return pl.pallas_call(
        functools.partial(body, n_sub=n_sub),
        grid=(rows // tm, dff // tf),
        in_specs=specs + ffn_specs,
        out_specs=row(d),
        out_shape=jax.ShapeDtypeStruct((rows, d), F32),
        scratch_shapes=[pltpu.VMEM((tm, d), BF16)],
        compiler_params=_params("parallel", "arbitrary"),
        name="ffn" if merge is None else "merge_ffn",
    )(*args, g, w_up, w_down)


def _inproj_kernel(x_ref, g_ref, w_ref, wlr_ref, gw_ref, gbias_ref, qg_ref, kg_ref, bd_ref, *outs,
                   tail_every):
    if tail_every:
        q_o, k_o, v_o, kt_o, vt_o, qb_o, kb_o, la_o, vb_o, gb_o = outs
    else:
        q_o, k_o, v_o, qb_o, kb_o, la_o, vb_o, gb_o = outs
    xn = _rms(x_ref[...], g_ref[0]).astype(BF16)

    def seg(a, b):
        return _mm(xn, w_ref[0, :, a:b])

    def head_norm(h, gain):
        sq = (h * h).astype(BF16)
        w = bd_ref.shape[0]
        ss = jnp.concatenate([_mm(sq[:, c:c + w], bd_ref[...]) for c in range(0, WA, w)], axis=1)
        return h * lax.rsqrt(ss * (1.0 / HEAD_DIM_A) + EPS) * gain

    q_o[...] = (head_norm(seg(0, WA), qg_ref[0]) * (HEAD_DIM_A ** -0.5)).astype(BF16)
    k = head_norm(seg(WA, 2 * WA), kg_ref[0])
    v = seg(2 * WA, 3 * WA)
    k_o[...] = k.astype(k_o.dtype)
    v_o[...] = v.astype(v_o.dtype)
    if tail_every:
        @pl.when(pl.program_id(0) % tail_every == tail_every - 1)
        def _():
            kt_o[0] = k.T
            vt_o[0] = v.T
    o = 3 * WA
    hb = seg(o, o + 2 * KB)
    qb_o[...] = hb[:, :KB] * (HEAD_K_B ** -0.5)
    kb_o[...] = hb[:, KB:]
    vb_o[...] = seg(o + 2 * KB, o + 2 * KB + VB).astype(BF16)
    gb_o[...] = seg(o + 2 * KB + VB, P_MAIN)
    lr = _mm(xn, wlr_ref[0])
    logit = _mm(lr.astype(BF16), gw_ref[0]) + gbias_ref[0]
    log_sig = jnp.minimum(logit, 0.0) - jnp.log1p(jnp.exp(-jnp.abs(logit)))
    la_o[...] = log_sig * (1.0 / GATE_NORM)


def _inproj(x, g, l, w_in, w_lr, gate_w, gate_b, qg, kg, i, bd, *, stream_len=None, tm=512):
    rows, d = x.shape
    tm = _row_tile(rows, tm)
    row = lambda w: pl.BlockSpec((tm, w), lambda r: (r, 0))
    kv_dtype = BF16 if stream_len else F32
    outs = [(WA, BF16), (WA, kv_dtype), (WA, kv_dtype), (KB, F32), (KB, F32), (KB, F32), (VB, BF16), (VB, F32)]
    out_specs = [row(w) for w, _ in outs]
    out_shape = [jax.ShapeDtypeStruct((rows, w), dt) for w, dt in outs]
    tail_every = None
    if stream_len:
        assert stream_len % tm == 0 and tm == PAD
        tail_every = stream_len // tm
        tail_spec = pl.BlockSpec((1, WA, tm), lambda r: (r // tail_every, 0, 0))
        tail_shape = jax.ShapeDtypeStruct((rows // stream_len, WA, tm), F32)
        out_specs[3:3] = [tail_spec, tail_spec]
        out_shape[3:3] = [tail_shape, tail_shape]
    return pl.pallas_call(
        functools.partial(_inproj_kernel, tail_every=tail_every),
        grid=(rows // tm,),
        in_specs=[row(d), _layer_spec(g, l), _layer_spec(w_in, i), _layer_spec(w_lr, i),
                  _layer_spec(gate_w, i), _layer_spec(gate_b, i), _layer_spec(qg, i),
                  _layer_spec(kg, i), _full_spec(bd)],
        out_specs=out_specs,
        out_shape=out_shape,
        compiler_params=_params("arbitrary"),
        name="inproj",
    )(x, g, w_in, w_lr, gate_w, gate_b, qg, kg, bd)


def _softmax_pv(scores, pv_fns):
    m = scores[0].max(axis=-1, keepdims=True)
    for s in scores[1:]:
        m = jnp.maximum(m, s.max(axis=-1, keepdims=True))
    l = 0.0
    pv = 0.0
    for s, fn in zip(scores, pv_fns):
        p = jnp.exp(s - m)
        l = l + p.sum(axis=-1, keepdims=True)
        pv = pv + fn(p.astype(BF16))
    return pv / l


def _stack_pair(q2, first_half):
    zero = jnp.zeros_like(q2)
    return jnp.concatenate([jnp.where(first_half, q2, zero), jnp.where(first_half, zero, q2)], axis=0)


GROUP = 4
GROUP_ROWS = GROUP * CHUNK
WINDOW = PAD + GROUP_ROWS


def _attn_prompt_kernel(q_ref, k_ref, v_ref, bias_ref, o_ref, *, ng):
    first_half = lax.broadcasted_iota(jnp.int32, (GROUP_ROWS, LANES), 1) < HEAD_DIM_A

    def group(r0, keys, bias_cols):
        for p in range(N_HEADS_A // 2):
            lanes = slice(p * LANES, (p + 1) * LANES)
            kw = k_ref[keys, lanes]
            vw = v_ref[keys, lanes]
            qs = _stack_pair(q_ref[pl.ds(r0, GROUP_ROWS), lanes], first_half)
            s = _mm_nt(qs, kw) + bias_ref[p, :, bias_cols]
            o = _softmax_pv([s], [lambda pb: _mm(pb, vw)])
            o_ref[pl.ds(r0, GROUP_ROWS), lanes] = jnp.where(
                first_half, o[:GROUP_ROWS], o[GROUP_ROWS:]).astype(BF16)

    n_ramp = min(ng, PAD // GROUP_ROWS)
    for g in range(n_ramp):
        n_keys = (g + 1) * GROUP_ROWS
        group(g * GROUP_ROWS, slice(0, n_keys), slice(WINDOW - n_keys, WINDOW))

    def full_group(g, carry):
        r0 = pl.multiple_of(g * GROUP_ROWS, GROUP_ROWS)
        group(r0, pl.ds(pl.multiple_of(r0 - PAD, GROUP_ROWS), WINDOW), slice(0, WINDOW))
        return carry

    lax.fori_loop(n_ramp, ng, full_group, 0, unroll=2)


def _attn_prompt(q, k, v, bias, nb, t):
    ng = t // GROUP_ROWS
    blk = pl.BlockSpec((t, WA), lambda b: (b, 0))
    return pl.pallas_call(
        functools.partial(_attn_prompt_kernel, ng=ng),
        grid=(nb,),
        in_specs=[blk, blk, blk, _full_spec(bias)],
        out_specs=blk,
        out_shape=jax.ShapeDtypeStruct((nb * t, WA), BF16),
        compiler_params=_params("parallel"),
        name="attn_prompt",
    )(q, k, v, bias)


def _attn_sample_kernel(q_ref, kn_ref, vn_ref, ckt_ref, cvt_ref, bias_c_ref, bias_n_ref, o_ref, *, ns, t):
    first_half = lax.broadcasted_iota(jnp.int32, (t, LANES), 1) < HEAD_DIM_A
    for s in range(ns):
        rows = slice(s * t, (s + 1) * t)
        for p in range(N_HEADS_A // 2):
            lanes = slice(p * LANES, (p + 1) * LANES)
            qs = _stack_pair(q_ref[rows, lanes], first_half)
            kct = ckt_ref[0, s, lanes, :].astype(BF16)
            vct = cvt_ref[0, s, lanes, :].astype(BF16)
            kn = kn_ref[rows, lanes].astype(BF16)
            vn = vn_ref[rows, lanes].astype(BF16)
            s_c = _mm(qs, kct) + bias_c_ref[p]
            s_n = _mm_nt(qs, kn) + bias_n_ref[p]
            o = _softmax_pv([s_c, s_n], [lambda pb: _mm_nt(pb, vct), lambda pb: _mm(pb, vn)])
            o_ref[rows, lanes] = jnp.where(first_half, o[:t], o[t:]).astype(BF16)


def _attn_sample(q, k_new, v_new, cache_kt, cache_vt, li, bias_c, bias_n, nb, t, *, ns=4):
    lc = cache_kt.shape[3]
    assert nb % ns == 0
    blk = pl.BlockSpec((ns * t, WA), lambda b: (b, 0))
    cblk = pl.BlockSpec((1, ns, WA, lc), lambda b: (li, b, 0, 0))
    return pl.pallas_call(
        functools.partial(_attn_sample_kernel, ns=ns, t=t),
        grid=(nb // ns,),
        in_specs=[blk, blk, blk, cblk, cblk, _full_spec(bias_c), _full_spec(bias_n)],
        out_specs=blk,
        out_shape=jax.ShapeDtypeStruct((nb * t, WA), BF16),
        compiler_params=_params("parallel"),
        name="attn_sample",
    )(q, k_new, v_new, cache_kt, cache_vt, bias_c, bias_n)


def _gla_kernel(q_ref, k_ref, v_ref, la_ref, gb_ref, og_ref, s0_ref, o_ref, sout_ref, st_scr,
                *, ns, nblk, ncb, c_len):
    n_pairs = N_HEADS_B // 2
    r_len = ncb * c_len

    @pl.when(pl.program_id(1) == 0)
    def _():
        for s in range(ns):
            for p in range(n_pairs):
                st_scr[s, p] = s0_ref[s, p * LANES:(p + 1) * LANES, :].T

    ri = lax.broadcasted_iota(jnp.int32, (r_len, r_len), 0)
    ci = lax.broadcasted_iota(jnp.int32, (r_len, r_len), 1)
    same_chunk = (ri // c_len) == (ci // c_len)
    causal = same_chunk & (ri >= ci)
    causal2 = jnp.concatenate([causal, causal], axis=0)
    tril = causal.astype(BF16)
    ones_bd = same_chunk.astype(BF16)
    row_chunk = lax.broadcasted_iota(jnp.int32, (r_len, HEAD_V_B), 0) // c_len
    first_half = lax.broadcasted_iota(jnp.int32, (r_len, LANES), 1) < HEAD_K_B
    first_half_sq = lax.broadcasted_iota(jnp.int32, (LANES, LANES), 1) < HEAD_K_B

    def one_block(s, r0):
        rows = pl.ds(r0, r_len)
        la = la_ref[s, rows, :]
        la_hi = la.astype(BF16)
        la_lo = (la - la_hi.astype(F32)).astype(BF16)
        b = _mm(tril, la_hi) + _mm(tril, la_lo)
        bl = _mm(ones_bd, la_hi) + _mm(ones_bd, la_lo)
        ref = 0.5 * bl
        q = q_ref[s, rows, :]
        k = k_ref[s, rows, :]
        qt = (q * jnp.exp(b - ref)).astype(BF16)
        kt = (k * jnp.exp(ref - b)).astype(BF16)
        qd = (q * jnp.exp(b)).astype(BF16)
        kd = (k * jnp.exp(bl - b)).astype(BF16)
        dl = jnp.exp(bl)
        for p in range(n_pairs):
            lanes = slice(p * LANES, (p + 1) * LANES)
            kt2 = kt[:, lanes]
            kd2 = kd[:, lanes]
            vlanes = [slice((2 * p + e) * HEAD_V_B, (2 * p + e + 1) * HEAD_V_B) for e in range(2)]
            vs = [v_ref[s, rows, vl] for vl in vlanes]
            a = jnp.where(causal2, _mm_nt(_stack_pair(qt[:, lanes], first_half), kt2), 0.0).astype(BF16)
            intra = [_mm(a[e * r_len:(e + 1) * r_len], vs[e]) for e in range(2)]
            qd_stack = _stack_pair(qd[:, lanes], first_half)
            v_bd = jnp.concatenate([jnp.where(row_chunk == c, vs[e], jnp.zeros_like(vs[e]))
                                    for e in range(2) for c in range(ncb)], axis=1)
            upd = _mm_tn(v_bd, kd2)
            st = st_scr[s, p]
            inter = [[], []]
            for c in range(ncb):
                q_c = jnp.concatenate([qd_stack[e * r_len + c * c_len:e * r_len + (c + 1) * c_len] for e in range(2)], axis=0)
                o_c = _mm_nt(q_c, st.astype(BF16))
                inter[0].append(o_c[:c_len])
                inter[1].append(o_c[c_len:])
                u0 = upd[c * HEAD_V_B:(c + 1) * HEAD_V_B]
                u1 = upd[(ncb + c) * HEAD_V_B:(ncb + c + 1) * HEAD_V_B]
                st = st * dl[c * c_len:c * c_len + 1, lanes] + jnp.where(first_half_sq, u0, u1)
            st_scr[s, p] = st
            for e in range(2):
                o_inter = inter[e][0] if ncb == 1 else jnp.concatenate(inter[e], axis=0)
                gate = gb_ref[s, rows, vlanes[e]]
                o = _rms(intra[e] + o_inter, og_ref[0]) * (gate * _sigmoid(gate))
                o_ref[s, rows, vlanes[e]] = o.astype(BF16)

    def block(j, carry):
        for s in range(ns):
            one_block(s, pl.multiple_of(j * r_len, r_len))
        return carry

    lax.fori_loop(0, nblk, block, 0)

    @pl.when(pl.program_id(1) == pl.num_programs(1) - 1)
    def _():
        for s in range(ns):
            for p in range(n_pairs):
                sout_ref[s, p * LANES:(p + 1) * LANES, :] = st_scr[s, p].T


def _gla(q, k, v, la, gb, og, li, s0, nb, t, c_len, *, ns, tb=1024, r_len=256):
    tb = _row_tile(t, tb)
    r_len = _row_tile(tb, r_len)
    assert nb % ns == 0 and r_len % c_len == 0
    kblk = pl.BlockSpec((ns, tb, KB), lambda b, j: (b, j, 0))
    vblk = pl.BlockSpec((ns, tb, VB), lambda b, j: (b, j, 0))
    sblk = pl.BlockSpec((ns, KB, HEAD_V_B), lambda b, j: (b, 0, 0))
    per_stream = lambda a: a.reshape(nb, t, a.shape[-1])
    o, s_out = pl.pallas_call(
        functools.partial(_gla_kernel, ns=ns, nblk=tb // r_len, ncb=r_len // c_len, c_len=c_len),
        grid=(nb // ns, t // tb),
        in_specs=[kblk, kblk, vblk, kblk, vblk, _layer_spec(og, li), sblk],
        out_specs=[vblk, sblk],
        out_shape=[jax.ShapeDtypeStruct((nb, t, VB), BF16),
                   jax.ShapeDtypeStruct((nb, KB, HEAD_V_B), F32)],
        scratch_shapes=[pltpu.VMEM((ns, N_HEADS_B // 2, LANES, LANES), F32)],
        compiler_params=_params("parallel", "arbitrary"),
        name="gla",
    )(per_stream(q), per_stream(k), per_stream(v), per_stream(la), per_stream(gb), og, s0)
    return o.reshape(nb * t, VB), s_out


def _conv_kernel(x_ref, init_ref, g_ref, w1_ref, b1_ref, wdw_ref, bdw_ref, lng_ref, lnb_ref,
                 w2_ref, b2_ref, o_ref, tail_ref, buf_scr, *, ns, tt):
    d = x_ref.shape[1]
    lo = CONV_HALO - (CONV_W - 1)

    @pl.when(pl.program_id(1) == 0)
    def _():
        for s in range(ns):
            buf_scr[s, lo:CONV_HALO, :] = init_ref[s]
            buf_scr[s, CONV_HALO + tt:CONV_HALO + tt + SUBLANES, :] = jnp.zeros((SUBLANES, d), F32)

    x = x_ref[...]
    u = _mm(_rms(x, g_ref[0]).astype(BF16), w1_ref[0]) + b1_ref[0]
    glu = u[:, :d] * _sigmoid(u[:, d:])
    ys = []
    for s in range(ns):
        buf_scr[s, CONV_HALO:CONV_HALO + tt, :] = glu[s * tt:(s + 1) * tt]
        y = bdw_ref[0]
        for r in range(SUBLANES):
            part = None
            for k in range(lo, lo + CONV_W):
                if k % SUBLANES == r:
                    term = wdw_ref[0, k - lo:k - lo + 1, :] * buf_scr[s, k - r:k - r + tt + SUBLANES, :]
                    part = term if part is None else part + term
            y = y + part[r:r + tt, :]
        ys.append(y)
        tail = buf_scr[s, tt:tt + CONV_HALO, :]
        tail_ref[s] = tail
        buf_scr[s, 0:CONV_HALO, :] = tail
    y = ys[0] if ns == 1 else jnp.concatenate(ys, axis=0)
    mu = jnp.mean(y, axis=-1, keepdims=True)
    yc = y - mu
    var = jnp.mean(yc * yc, axis=-1, keepdims=True)
    yn = yc * lax.rsqrt(var + EPS) * lng_ref[0] + lnb_ref[0]
    z = (yn * _sigmoid(yn)).astype(BF16)
    o_ref[...] = x + _mm(z, w2_ref[0]) + b2_ref[0]


def _conv(x, init, g, l, conv_params, j, nb, t, *, tt=512, ns=8):
    rows, d = x.shape
    tt = _row_tile(t, tt)
    nt = t // tt
    ns = ns if nt == 1 else 1
    assert nb % ns == 0
    xblk = pl.BlockSpec((ns * tt, d), lambda b, i: (b * nt + i, 0))
    return pl.pallas_call(
        functools.partial(_conv_kernel, ns=ns, tt=tt),
        grid=(nb // ns, nt),
        in_specs=[xblk, pl.BlockSpec((ns, CONV_W - 1, d), lambda b, i: (b, 0, 0)), _layer_spec(g, l)]
                 + [_layer_spec(a, j) for a in conv_params],
        out_specs=[xblk, pl.BlockSpec((ns, CONV_HALO, d), lambda b, i: (b, 0, 0))],
        out_shape=[jax.ShapeDtypeStruct((rows, d), F32),
                   jax.ShapeDtypeStruct((nb, CONV_HALO, d), F32)],
        scratch_shapes=[pltpu.VMEM((ns, CONV_HALO + tt + SUBLANES, d), F32)],
        compiler_params=_params("parallel", "arbitrary"),
        name="conv",
    )(x, init, g, *conv_params)


def _band_bias(table):
    n_far = PAD + CHUNK - REL_MAX
    ext = jnp.concatenate([jnp.repeat(table[..., -1:], n_far, axis=-1), table[..., -2::-1]], axis=-1)
    period = ext.shape[-1] + 1
    ext = jnp.pad(ext, ((0, 0), (0, 0), (0, 1)))
    wrapped = jnp.tile(ext, (1, 1, CHUNK + 1))[..., :CHUNK * (period + 1)]
    wrapped = wrapped.reshape(ext.shape[:2] + (CHUNK, period + 1))
    return wrapped[:, :, ::-1, :BAND]


def _group_bias(bias):
    per_chunk = [jnp.pad(bias, ((0, 0), (0, 0), (c * CHUNK, (GROUP - 1 - c) * CHUNK)), constant_values=-1e30)
                 for c in range(GROUP)]
    stacked = jnp.stack(per_chunk, axis=1)
    return stacked.reshape(N_HEADS_A // 2, 2 * GROUP_ROWS, WINDOW)


def _positions_minor(a):
    n_l, n_b, n_t = a.shape[:3]
    return jnp.transpose(a, (0, 1, 3, 4, 2)).reshape(n_l, n_b, WA, n_t)


def _heads_minor(a):
    n_l, n_b, _, n_t = a.shape
    return jnp.transpose(a.reshape(n_l, n_b, N_HEADS_A, HEAD_DIM_A, n_t), (0, 1, 4, 2, 3))


def kernel(x_prompt, x_sample, cache_a_k, cache_a_v, state_b, cache_c, norm_mix_g, norm_ffn_g, w_in_ab, a_q_norm_g, a_k_norm_g, a_rel_bias, b_gate_w_up, b_gate_b, b_out_norm_g, w_out_ab, c_w_pw1, c_b_pw1, c_w_dw, c_b_dw, c_ln_g, c_ln_b, c_w_pw2, c_b_pw2, w_ffn_up, w_ffn_down):
    nbp, tp, d = x_prompt.shape
    nbs, ts, _ = x_sample.shape
    depth = norm_mix_g.shape[0]
    lp = min(PAD, tp)
    lc = cache_a_k.shape[2]
    assert lc == PAD and lp == PAD and ts <= CHUNK and tp % CHUNK == 0
    xp = x_prompt.reshape(nbp * tp, d)
    xs = x_sample.reshape(nbs * ts, d)
    vec = lambda a: a.reshape(a.shape[0], 1, a.shape[1])

    head_id = np.arange(MXU_TILE) // HEAD_DIM_A
    bd = jnp.asarray(head_id[:, None] == head_id[None, :], BF16)
    g_mix, g_ffn = vec(norm_mix_g), vec(norm_ffn_g)
    w_up, w_down = w_ffn_up.astype(BF16), w_ffn_down.astype(BF16)
    w_in = w_in_ab.astype(BF16)
    w_lr = jnp.pad(w_in_ab[:, :, P_MAIN:], ((0, 0), (0, 0), (0, LANES - GATE_RANK))).astype(BF16)
    gate_w = jnp.pad(b_gate_w_up, ((0, 0), (0, LANES - GATE_RANK), (0, 0))).astype(BF16)
    proj = (w_in, w_lr, gate_w, vec(b_gate_b),
            vec(jnp.tile(a_q_norm_g, (1, N_HEADS_A))), vec(jnp.tile(a_k_norm_g, (1, N_HEADS_A))))
    og = vec(b_out_norm_g)
    w_out = w_out_ab.astype(BF16)
    conv_params = (c_w_pw1.astype(BF16), vec(c_b_pw1), c_w_dw, vec(c_b_dw), vec(c_ln_g), vec(c_ln_b),
                   c_w_pw2.astype(BF16), vec(c_b_pw2))
    bias = _band_bias(a_rel_bias)
    cache_kt, cache_vt = _positions_minor(cache_a_k), _positions_minor(cache_a_v)
    pair_rows = lambda a: a.reshape(N_HEADS_A // 2, 2 * ts, a.shape[-1])

    akp, avp, aks, avs, sbp, sbs, ccp, ccs = [], [], [], [], [], [], [], []
    for l in range(depth):
        if l % 2 == 0:
            i = l // 2
            q, k, v, kt, vt, qb, kb, la, vb, gb = _inproj(xp, g_mix, l, *proj, i, bd, stream_len=tp)
            oa = _attn_prompt(q, k, v, _group_bias(bias[i]), nbp, tp)
            ob, sb = _gla(qb, kb, vb, la, gb, og, i, jnp.zeros((nbp, KB, HEAD_V_B), F32), nbp, tp, CHUNK,
                          ns=min(nbp, 2))
            akp.append(kt)
            avp.append(vt)
            sbp.append(sb.reshape(nbp, N_HEADS_B, HEAD_K_B, HEAD_V_B))
            merge_p = (oa, ob, w_out, i)

            q, k, v, qb, kb, la, vb, gb = _inproj(xs, g_mix, l, *proj, i, bd)
            oa = _attn_sample(q, k, v, cache_kt, cache_vt, i,
                              pair_rows(bias[i, :, :ts, :lc]), pair_rows(bias[i, :, :ts, lc:lc + ts]), nbs, ts,
                              ns=min(nbs, 4))
            ob, sb = _gla(qb, kb, vb, la, gb, og, i, state_b[i].reshape(nbs, KB, HEAD_V_B), nbs, ts, ts,
                          ns=min(nbs, 8))
            aks.append(k.reshape(nbs, ts, N_HEADS_A, HEAD_DIM_A))
            avs.append(v.reshape(nbs, ts, N_HEADS_A, HEAD_DIM_A))
            sbs.append(sb.reshape(nbs, N_HEADS_B, HEAD_K_B, HEAD_V_B))
            merge_s = (oa, ob, w_out, i)
        else:
            merge_p = merge_s = None
            j = l // 2
            xp, tail = _conv(xp, jnp.zeros((nbp, CONV_W - 1, d), F32), g_mix, l, conv_params, j, nbp, tp)
            ccp.append(tail[:, CONV_HALO - (CONV_W - 1):])
            xs, tail = _conv(xs, cache_c[j], g_mix, l, conv_params, j, nbs, ts, ns=min(nbs, 8))
            ccs.append(tail[:, CONV_HALO - (CONV_W - 1):])
        xp = _ffn(xp, g_ffn, w_up, w_down, l, merge_p)
        xs = _ffn(xs, g_ffn, w_up, w_down, l, merge_s)
    return (xp.reshape(nbp, tp, d), xs.reshape(nbs, ts, d),
            _heads_minor(jnp.stack(akp)), _heads_minor(jnp.stack(avp)), jnp.stack(aks), jnp.stack(avs),
            jnp.stack(sbp), jnp.stack(sbs), jnp.stack(ccp), jnp.stack(ccs))
```
